```python
import jax, jax.numpy as jnp
from jax import lax
import numpy as np

D_MODEL = 1024
BATCH = 2
SEQ = 8192
DEPTH = 4
DEC_BATCH = 128
DEC_SEQ = 4
PAST_LEN = 2048
PAGE_SIZE = 128

HEAD_DIM = 64
H_RWKV = 6
H_FOX = 5
H_SB = 5
D_RWKV = H_RWKV * HEAD_DIM
D_FOX = H_FOX * HEAD_DIM
D_SB = H_SB * HEAD_DIM
D_MIX = D_RWKV + D_FOX + D_SB
D_DECAY_LORA = 64
D_AAA_LORA = 64
N_SHIFT = 4 * D_RWKV + D_DECAY_LORA + D_AAA_LORA
N_FOX_COLS = 3 * D_FOX + H_FOX
N_SB_COLS = 3 * D_SB
N_IN = N_SHIFT + N_FOX_COLS + N_SB_COLS
D_FF = -(-8 * D_MODEL // (3 * 256)) * 256
Q_BLOCK = 128
KV_SLOTS = 4
RMS_EPS = 1e-6
GN_EPS = 64e-5
ATTN_SCALE = HEAD_DIM ** -0.5

kernel_name = "hymba_rwkv7_fox_stickbreak_decoder_step"


def rmsnorm(x, g):
    xf = x.astype(jnp.float32)
    y = xf * lax.rsqrt(jnp.mean(xf * xf, axis=-1, keepdims=True) + RMS_EPS)
    return (y * g).astype(x.dtype)


def to_blocks(a):
    b, t = a.shape[:2]
    return jnp.swapaxes(a.reshape(b, t // Q_BLOCK, Q_BLOCK, *a.shape[2:]), 0, 1)


def from_blocks(a):
    a = jnp.swapaxes(a, 0, 1)
    return a.reshape(a.shape[0], -1, *a.shape[3:])


def project_heads(xn, w_in_l, b_f, q_g, k_g):
    b, t, _ = xn.shape
    u = xn @ w_in_l
    heads = lambda z: z.reshape(b, t, -1, HEAD_DIM)
    o = N_SHIFT
    feats = u[..., :o]
    fq = heads(u[..., o:o + D_FOX])
    fk = heads(u[..., o + D_FOX:o + 2 * D_FOX])
    fv = heads(u[..., o + 2 * D_FOX:o + 3 * D_FOX])
    f_pre = u[..., o + 3 * D_FOX:o + N_FOX_COLS]
    o = N_SHIFT + N_FOX_COLS
    sq = heads(u[..., o:o + D_SB])
    sk = heads(u[..., o + D_SB:o + 2 * D_SB])
    sv = heads(u[..., o + 2 * D_SB:o + 3 * D_SB])
    fq = rmsnorm(fq, q_g)
    fk = rmsnorm(fk, k_g)
    logf = jax.nn.log_sigmoid((f_pre + b_f).astype(jnp.float32))
    return feats, fq, fk, fv, logf, sq, sk, sv


def rwkv_time_mix(feats, prev, S0, mu, w0, w2, a0, a2, k_k, k_a, r_k, ln_w, ln_b):
    b, t, _ = feats.shape
    f32 = jnp.float32
    shifted = jnp.concatenate([prev[:, None, :].astype(feats.dtype), feats[:, :-1]], axis=1)
    m = feats + (shifted - feats) * mu
    r, k, v, g = (m[..., i * D_RWKV:(i + 1) * D_RWKV] for i in range(4))
    w_lo = m[..., 4 * D_RWKV:4 * D_RWKV + D_DECAY_LORA]
    a_lo = m[..., 4 * D_RWKV + D_DECAY_LORA:]
    w = -jax.nn.softplus(-(w0 + jnp.tanh(w_lo) @ w2)) - 0.5
    decay = jnp.exp(-jnp.exp(w.astype(f32)))
    a = jax.nn.sigmoid(a0 + a_lo @ a2)
    kk = (k * k_k).astype(f32).reshape(b, t, H_RWKV, HEAD_DIM)
    kk = kk / jnp.maximum(jnp.sqrt(jnp.sum(kk * kk, axis=-1, keepdims=True)), 1e-12)
    k = k * (1.0 + (a - 1.0) * k_a)
    hd = lambda z: z.astype(f32).reshape(b, t, H_RWKV, HEAD_DIM)
    r_h, k_h, v_h, w_h, a_h = hd(r), hd(k), hd(v), hd(decay), hd(a)

    def step(S, inp):
        r_t, w_t, k_t, v_t, kk_t, a_t = inp
        sa = jnp.einsum('bhvk,bhk->bhv', S, -kk_t)
        S = (S * w_t[:, :, None, :] + sa[..., None] * (kk_t * a_t)[:, :, None, :]
             + v_t[..., None] * k_t[:, :, None, :])
        return S, jnp.einsum('bhvk,bhk->bhv', S, r_t)

    xs = tuple(jnp.moveaxis(z, 1, 0) for z in (r_h, w_h, k_h, v_h, kk, a_h))
    S_T, ys = lax.scan(step, S0.astype(f32), xs)
    y = jnp.moveaxis(ys, 0, 1)
    mean = jnp.mean(y, axis=-1, keepdims=True)
    var = jnp.mean(jnp.square(y - mean), axis=-1, keepdims=True)
    y = ((y - mean) * lax.rsqrt(var + GN_EPS)).reshape(b, t, D_RWKV) * ln_w + ln_b
    bonus = jnp.sum(r_h * k_h * r_k, axis=-1, keepdims=True) * v_h
    out = (y + bonus.reshape(b, t, D_RWKV)) * jax.nn.sigmoid(g.astype(f32))
    return out.astype(feats.dtype), S_T, feats[:, -1]


def fox_attend(q, cum_q, pos_q, k, v, cum_k, pos_k):
    s = jnp.einsum('bqhd,bkhd->bhqk', q, k).astype(jnp.float32) * ATTN_SCALE
    s = s + jnp.swapaxes(cum_q, 1, 2)[..., :, None] - jnp.swapaxes(cum_k, 1, 2)[..., None, :]
    s = jnp.where(pos_k[None, :] <= pos_q[:, None], s, -jnp.inf)
    p = jax.nn.softmax(s, axis=-1)
    return jnp.einsum('bhqk,bkhd->bqhd', p.astype(v.dtype), v)


def sb_attend(q, pos_q, k, v, pos_k):
    z = jnp.einsum('bqhd,bkhd->bhqk', q, k).astype(jnp.float32) * ATTN_SCALE
    mask = pos_k[None, :] < pos_q[:, None]
    log_1mb = jnp.where(mask, jax.nn.log_sigmoid(-z), 0.0)
    between = lax.cumsum(log_1mb, axis=3, reverse=True) - log_1mb
    A = jnp.where(mask, jnp.exp(jax.nn.log_sigmoid(z) + between), 0.0)
    return jnp.einsum('bhqk,bkhd->bqhd', A.astype(v.dtype), v)


def merge_and_ffn(x, y_a, y_b, y_c, w_out_l, norm2_g_l, wg, wu, wd):
    b, t, _ = x.shape
    mix = jnp.concatenate([y_a, y_b.reshape(b, t, D_FOX), y_c.reshape(b, t, D_SB)], axis=-1)
    h = x + mix @ w_out_l
    hn = rmsnorm(h, norm2_g_l)
    return h + (jax.nn.silu(hn @ wg) * (hn @ wu)) @ wd


def setup_inputs(seed: int = 0) -> dict:
    key = jax.random.key(seed)
    ks = jax.random.split(key, 32)
    nrm = lambda i, shape: jax.random.normal(ks[i], shape, jnp.float32)
    n_pages = PAST_LEN // PAGE_SIZE
    n_used = DEC_BATCH * n_pages
    n_phys = n_used + max(1, n_used // 4)
    page_table = jax.random.permutation(ks[0], n_phys)[:n_used].astype(jnp.int32).reshape(DEC_BATCH, n_pages)
    return {
        "x_prompt": nrm(1, (BATCH, SEQ, D_MODEL)),
        "x_sample": nrm(2, (DEC_BATCH, DEC_SEQ, D_MODEL)),
        "cache_kv": nrm(3, (DEPTH, n_phys, PAGE_SIZE, KV_SLOTS, H_FOX, HEAD_DIM)),
        "cache_logf": jax.nn.log_sigmoid(2.0 + nrm(4, (DEPTH, n_phys, PAGE_SIZE, H_FOX))),
        "state_wkv": 0.5 * nrm(5, (DEPTH, DEC_BATCH, H_RWKV, HEAD_DIM, HEAD_DIM)),
        "state_shift": nrm(6, (DEPTH, DEC_BATCH, N_SHIFT)),
        "page_table": page_table,
        "norm1_g": 1.0 + 0.02 * nrm(7, (DEPTH, D_MODEL)),
        "w_in": nrm(8, (DEPTH, D_MODEL, N_IN)) * D_MODEL ** -0.5,
        "shift_mu": jax.random.uniform(ks[9], (DEPTH, N_SHIFT), jnp.float32),
        "rwkv_w0": -2.5 + nrm(10, (DEPTH, D_RWKV)),
        "rwkv_w2": 0.1 * nrm(11, (DEPTH, D_DECAY_LORA, D_RWKV)) * D_DECAY_LORA ** -0.5,
        "rwkv_a0": 0.1 * nrm(12, (DEPTH, D_RWKV)),
        "rwkv_a2": nrm(13, (DEPTH, D_AAA_LORA, D_RWKV)) * D_AAA_LORA ** -0.5,
        "rwkv_k_k": 0.85 + 0.02 * nrm(14, (DEPTH, D_RWKV)),
        "rwkv_k_a": 1.0 + 0.02 * nrm(15, (DEPTH, D_RWKV)),
        "rwkv_r_k": 0.1 * nrm(16, (DEPTH, H_RWKV, HEAD_DIM)),
        "rwkv_ln_w": 1.0 + 0.02 * nrm(17, (DEPTH, D_RWKV)),
        "rwkv_ln_b": 0.02 * nrm(18, (DEPTH, D_RWKV)),
        "fox_b_f": 2.0 + 0.5 * nrm(19, (DEPTH, H_FOX)),
        "fox_q_g": 1.0 + 0.02 * nrm(20, (DEPTH, HEAD_DIM)),
        "fox_k_g": 1.0 + 0.02 * nrm(21, (DEPTH, HEAD_DIM)),
        "w_out": nrm(22, (DEPTH, D_MIX, D_MODEL)) * D_MIX ** -0.5,
        "norm2_g": 1.0 + 0.02 * nrm(23, (DEPTH, D_MODEL)),
        "ffn_w_gate": nrm(24, (DEPTH, D_MODEL, D_FF)) * D_MODEL ** -0.5,
        "ffn_w_up": nrm(25, (DEPTH, D_MODEL, D_FF)) * D_MODEL ** -0.5,
        "ffn_w_down": nrm(26, (DEPTH, D_FF, D_MODEL)) * D_FF ** -0.5,
    }


def reference(x_prompt, x_sample, cache_kv, cache_logf, state_wkv, state_shift, page_table,
              norm1_g, w_in, shift_mu, rwkv_w0, rwkv_w2, rwkv_a0, rwkv_a2, rwkv_k_k, rwkv_k_a,
              rwkv_r_k, rwkv_ln_w, rwkv_ln_b, fox_b_f, fox_q_g, fox_k_g, w_out, norm2_g,
              ffn_w_gate, ffn_w_up, ffn_w_down):
    n_past = page_table.shape[1] * cache_kv.shape[2]
    bp, tp = x_prompt.shape[:2]
    bs, ts = x_sample.shape[:2]
    pos_p = jnp.arange(tp, dtype=jnp.int32)
    pos_k_s = jnp.arange(n_past + ts, dtype=jnp.int32)
    pos_q_s = pos_k_s[n_past:]
    xp, xs = x_prompt, x_sample
    kv_p, lf_p, wkv_p, sh_p = [], [], [], []
    kv_s, lf_s, wkv_s, sh_s = [], [], [], []
    for l in range(DEPTH):
        rw = (shift_mu[l], rwkv_w0[l], rwkv_w2[l], rwkv_a0[l], rwkv_a2[l], rwkv_k_k[l],
              rwkv_k_a[l], rwkv_r_k[l], rwkv_ln_w[l], rwkv_ln_b[l])
        ffn = (w_out[l], norm2_g[l], ffn_w_gate[l], ffn_w_up[l], ffn_w_down[l])

        xn = rmsnorm(xp, norm1_g[l])
        feats, fq, fk, fv, logf, sq, sk, sv = project_heads(xn, w_in[l], fox_b_f[l], fox_q_g[l], fox_k_g[l])
        y_a, S_p, last_p = rwkv_time_mix(
            feats, jnp.zeros((bp, N_SHIFT), feats.dtype),
            jnp.zeros((bp, H_RWKV, HEAD_DIM, HEAD_DIM), jnp.float32), *rw)
        cum = jnp.cumsum(logf, axis=1)
        y_b = from_blocks(lax.map(
            lambda blk: fox_attend(blk[0], blk[1], blk[2], fk, fv, cum, pos_p),
            (to_blocks(fq), to_blocks(cum), pos_p.reshape(-1, Q_BLOCK))))
        y_c = from_blocks(lax.map(
            lambda blk: sb_attend(blk[0], blk[1], sk, sv, pos_p),
            (to_blocks(sq), pos_p.reshape(-1, Q_BLOCK))))
        xp = merge_and_ffn(xp, y_a, y_b, y_c, *ffn)
        kv_p.append(jnp.stack([fk, fv, sk, sv], axis=2))
        lf_p.append(logf.astype(cache_logf.dtype))
        wkv_p.append(S_p.astype(state_wkv.dtype))
        sh_p.append(last_p)

        xn = rmsnorm(xs, norm1_g[l])
        feats, fq, fk, fv, logf, sq, sk, sv = project_heads(xn, w_in[l], fox_b_f[l], fox_q_g[l], fox_k_g[l])
        y_a, S_s, last_s = rwkv_time_mix(feats, state_shift[l], state_wkv[l], *rw)
        past = cache_kv[l][page_table].reshape(bs, n_past, KV_SLOTS, H_FOX, HEAD_DIM)
        lf_past = cache_logf[l][page_table].reshape(bs, n_past, H_FOX).astype(jnp.float32)
        cum = jnp.cumsum(jnp.concatenate([lf_past, logf], axis=1), axis=1)
        k_b = jnp.concatenate([past[:, :, 0].astype(fk.dtype), fk], axis=1)
        v_b = jnp.concatenate([past[:, :, 1].astype(fv.dtype), fv], axis=1)
        k_c = jnp.concatenate([past[:, :, 2].astype(sk.dtype), sk], axis=1)
        v_c = jnp.concatenate([past[:, :, 3].astype(sv.dtype), sv], axis=1)
        y_b = fox_attend(fq, cum[:, n_past:], pos_q_s, k_b, v_b, cum, pos_k_s)
        y_c = sb_attend(sq, pos_q_s, k_c, v_c, pos_k_s)
        xs = merge_and_ffn(xs, y_a, y_b, y_c, *ffn)
        kv_s.append(jnp.stack([fk, fv, sk, sv], axis=2))
        lf_s.append(logf.astype(cache_logf.dtype))
        wkv_s.append(S_s.astype(state_wkv.dtype))
        sh_s.append(last_s)

    return (xp, xs,
            jnp.stack(kv_p), jnp.stack(lf_p), jnp.stack(wkv_p), jnp.stack(sh_p),
            jnp.stack(kv_s), jnp.stack(lf_s), jnp.stack(wkv_s), jnp.stack(sh_s))
```

```python
import functools

import numpy as np
import jax
import jax.numpy as jnp
from jax import lax
from jax.experimental import pallas as pl
from jax.experimental.pallas import tpu as pltpu

F32 = jnp.float32
BF16 = jnp.bfloat16

D_MODEL = 1024
HEAD_DIM = 64
H_RWKV = 6
H_ATT = 5
D_RWKV = H_RWKV * HEAD_DIM
D_ATT = H_ATT * HEAD_DIM
D_LORA = 64
N_SHIFT = 4 * D_RWKV + 2 * D_LORA
N_FOX_COLS = 3 * D_ATT + H_ATT
D_FF = 2816
KV_W = 4 * D_ATT
RMS_EPS = 1e-6
GN_EPS = 64e-5
ATTN_SCALE = HEAD_DIM ** -0.5

LANE = 128
QPAD = H_ATT * LANE
N_PAIR = H_RWKV // 2

C_FEATS = 0
C_KV = N_SHIFT
C_QF = C_KV + KV_W
C_QS = C_QF + QPAD
C_F = C_QS + QPAD
N_PROJ = C_F + LANE

VMEM_LIMIT = 56 * 1024 * 1024


def _cparams(sem):
    return pltpu.CompilerParams(dimension_semantics=sem, vmem_limit_bytes=VMEM_LIMIT)


def _dot(a, b):
    return jnp.dot(a, b, preferred_element_type=F32)


def _split3(x):
    hi = x.astype(BF16)
    r1 = x - hi.astype(F32)
    mid = r1.astype(BF16)
    lo = (r1 - mid.astype(F32)).astype(BF16)
    return hi, mid, lo


def _dot3_right(x, m):
    hi, mid, lo = _split3(x)
    return _dot(hi, m) + _dot(mid, m) + _dot(lo, m)


def _dot3_left(m, x):
    hi, mid, lo = _split3(x)
    return _dot(m, hi) + _dot(m, mid) + _dot(m, lo)


def _dot2_right(x, m):
    hi = x.astype(BF16)
    lo = (x - hi.astype(F32)).astype(BF16)
    return _dot(hi, m) + _dot(lo, m)


def _softplus(x):
    return jnp.maximum(x, 0.0) + jnp.log1p(jnp.exp(-jnp.abs(x)))


def _sigmoid(x):
    return 1.0 / (1.0 + jnp.exp(-x))


def _proj_body(x_ref, g_ref, w_ref, bdk_ref, bdq_ref, kg_ref, qg_ref, bf_ref,
               feats_ref, kv_ref, kvb_ref, qf_ref, qs_ref, lf_ref):
    x = x_ref[...]
    ms = jnp.mean(x * x, axis=-1, keepdims=True)
    xn = (x * lax.rsqrt(ms + RMS_EPS) * g_ref[...]).astype(BF16)

    feats_ref[...] = _dot(xn, w_ref[:, C_FEATS:C_KV])

    ukv = _dot(xn, w_ref[:, C_KV:C_QF])
    ufk = ukv[:, :3 * LANE]
    msk = _dot3_right(ufk * ufk, bdk_ref[...])
    col = lax.broadcasted_iota(jnp.int32, ufk.shape, 1)
    kv0 = jnp.where(col < D_ATT, ufk * lax.rsqrt(msk + RMS_EPS) * kg_ref[...], ufk)
    kv_ref[:, :3 * LANE] = kv0
    kv_ref[:, 3 * LANE:] = ukv[:, 3 * LANE:]
    kvb_ref[:, :3 * LANE] = kv0.astype(BF16)
    kvb_ref[:, 3 * LANE:] = ukv[:, 3 * LANE:].astype(BF16)

    uqf = _dot(xn, w_ref[:, C_QF:C_QS])
    msq = _dot3_right(uqf * uqf, bdq_ref[...])
    qf_ref[...] = (uqf * lax.rsqrt(msq + RMS_EPS) * qg_ref[...] * ATTN_SCALE).astype(BF16)

    uqs = _dot(xn, w_ref[:, C_QS:C_F])
    qs_ref[...] = (uqs * ATTN_SCALE).astype(BF16)

    uf = _dot(xn, w_ref[:, C_F:N_PROJ]) + bf_ref[...]
    lf_ref[...] = -_softplus(-uf)


def _proj(x, g, w, bdk, bdq, kg, qg, bf, tm):
    n = x.shape[0]
    full = lambda a: pl.BlockSpec(a.shape, lambda i: (0,) * a.ndim)
    row = lambda c: pl.BlockSpec((tm, c), lambda i: (i, 0))
    return pl.pallas_call(
        _proj_body,
        grid=(n // tm,),
        in_specs=[row(D_MODEL), full(g), full(w), full(bdk), full(bdq), full(kg), full(qg), full(bf)],
        out_specs=[row(N_SHIFT), row(KV_W), row(KV_W), row(QPAD), row(QPAD), row(LANE)],
        out_shape=[jax.ShapeDtypeStruct((n, N_SHIFT), F32),
                   jax.ShapeDtypeStruct((n, KV_W), F32),
                   jax.ShapeDtypeStruct((n, KV_W), BF16),
                   jax.ShapeDtypeStruct((n, QPAD), BF16),
                   jax.ShapeDtypeStruct((n, QPAD), BF16),
                   jax.ShapeDtypeStruct((n, LANE), F32)],
        compiler_params=_cparams(("arbitrary",)),
        name="proj",
    )(x, g, w, bdk, bdq, kg, qg, bf)


def _prep_body(f_ref, sh_ref, mu_ref, w0_ref, w2_ref, a0_ref, a2_ref, kk_ref, ka_ref, rk_ref, bd_ref,
               r_ref, w_ref, k_ref, nkk_ref, kka_ref, vt_ref, bonus_ref, gate_ref, *, transpose_v):
    f = f_ref[0]
    m = f + (sh_ref[0] - f) * mu_ref[...]
    r = m[:, 0:D_RWKV]
    k = m[:, D_RWKV:2 * D_RWKV]
    v = m[:, 2 * D_RWKV:3 * D_RWKV]
    g = m[:, 3 * D_RWKV:4 * D_RWKV]
    lo = m[:, 4 * D_RWKV:N_SHIFT]
    lw = _dot(jnp.tanh(lo).astype(BF16), w2_ref[...])
    la = _dot(lo.astype(BF16), a2_ref[...])
    w = -_softplus(-(w0_ref[...] + lw)) - 0.5
    decay = jnp.exp(-jnp.exp(w))
    a = _sigmoid(a0_ref[...] + la)
    kk = k * kk_ref[...]
    ss = _dot3_right(kk * kk, bd_ref[...])
    kk = kk / jnp.maximum(jnp.sqrt(ss), 1e-12)
    k2 = k * (1.0 + (a - 1.0) * ka_ref[...])
    rks = _dot3_right(r * k2 * rk_ref[...], bd_ref[...])
    r_ref[0] = r
    w_ref[0] = decay
    k_ref[0] = k2
    nkk_ref[0] = -kk
    kka_ref[0] = kk * a
    vt_ref[0] = v.T if transpose_v else v
    bonus_ref[0] = rks * v
    gate_ref[0] = _sigmoid(g)


def _prep(feats, shifted, mu, w0, w2p, a0, a2p, k_k, k_a, r_k, bd, tm, transpose_v):
    b, t, _ = feats.shape
    vspec = pl.BlockSpec((1, D_RWKV, tm), lambda i, j: (i, 0, j)) if transpose_v else None
    vshape = jax.ShapeDtypeStruct((b, D_RWKV, t), F32) if transpose_v else None
    full = lambda a: pl.BlockSpec(a.shape, lambda i, j: (0,) * a.ndim)
    blk = lambda c: pl.BlockSpec((1, tm, c), lambda i, j: (i, j, 0))
    o384 = jax.ShapeDtypeStruct((b, t, D_RWKV), F32)
    return pl.pallas_call(
        functools.partial(_prep_body, transpose_v=transpose_v),
        grid=(b, t // tm),
        in_specs=[blk(N_SHIFT), blk(N_SHIFT)] + [full(a) for a in (mu, w0, w2p, a0, a2p, k_k, k_a, r_k, bd)],
        out_specs=[blk(D_RWKV)] * 5 + [vspec or blk(D_RWKV)] + [blk(D_RWKV)] * 2,
        out_shape=[o384] * 5 + [vshape or o384] + [o384] * 2,
        compiler_params=_cparams(("arbitrary", "arbitrary")),
        name="rwkv_prep",
    )(feats, shifted, mu, w0, w2p, a0, a2p, k_k, k_a, r_k, bd)


def _scan_body(r_ref, w_ref, k_ref, nkk_ref, kka_ref, vt_ref, s0_ref, yt_ref, sT_ref, s_scr, *, nb, tc):
    c = pl.program_id(1)

    @pl.when(c == 0)
    def _():
        s_scr[...] = s0_ref[...]

    lane = lax.broadcasted_iota(jnp.int32, (HEAD_DIM, LANE), 1)
    left = lane < HEAD_DIM
    tlane = lax.broadcasted_iota(jnp.int32, (HEAD_DIM, tc), 1)
    yt_ref[...] = jnp.zeros(yt_ref.shape, F32)

    def seg_sum(z):
        za = jnp.sum(jnp.where(left, z, 0.0), axis=1, keepdims=True)
        zb = jnp.sum(jnp.where(left, 0.0, z), axis=1, keepdims=True)
        return jnp.where(left, za, zb)

    grp = min(8, tc)

    def group(i, carry):
        t0 = pl.multiple_of(i * grp, grp)
        for b in range(nb):
            for j in range(N_PAIR):
                ls = pl.ds(j * LANE, LANE)
                ra = pl.ds(j * LANE, HEAD_DIM)
                rb = pl.ds(j * LANE + HEAD_DIM, HEAD_DIM)
                rows = [ref[b, pl.ds(t0, grp), ls] for ref in (nkk_ref, w_ref, kka_ref, k_ref, r_ref)]
                s = s_scr[b, j]
                for u in range(grp):
                    nkk, w, kka, k, r = (x[u:u + 1, :] for x in rows)
                    hit = tlane == t0 + u
                    sa = seg_sum(s * nkk)
                    va = jnp.sum(jnp.where(hit, vt_ref[b, ra, :], 0.0), axis=1, keepdims=True)
                    vb = jnp.sum(jnp.where(hit, vt_ref[b, rb, :], 0.0), axis=1, keepdims=True)
                    vcol = jnp.where(left, va, vb)
                    s = s * w + sa * kka + vcol * k
                    z = s * r
                    ya = jnp.sum(jnp.where(left, z, 0.0), axis=1, keepdims=True)
                    yb = jnp.sum(jnp.where(left, 0.0, z), axis=1, keepdims=True)
                    yt_ref[b, ra, :] = jnp.where(hit, ya, yt_ref[b, ra, :])
                    yt_ref[b, rb, :] = jnp.where(hit, yb, yt_ref[b, rb, :])
                s_scr[b, j] = s
        return carry

    lax.fori_loop(0, tc // grp, group, 0)

    @pl.when(c == pl.num_programs(1) - 1)
    def _():
        sT_ref[...] = s_scr[...]


def _scan(r, w, k, nkk, kka, vt, s0, nb, tc):
    b, t, _ = r.shape
    blk = pl.BlockSpec((nb, tc, D_RWKV), lambda i, c: (i, c, 0))
    tblk = pl.BlockSpec((nb, D_RWKV, tc), lambda i, c: (i, 0, c))
    sblk = pl.BlockSpec((nb, N_PAIR, HEAD_DIM, LANE), lambda i, c: (i, 0, 0, 0))
    return pl.pallas_call(
        functools.partial(_scan_body, nb=nb, tc=tc),
        grid=(b // nb, t // tc),
        in_specs=[blk] * 5 + [tblk, sblk],
        out_specs=[tblk, sblk],
        out_shape=[jax.ShapeDtypeStruct((b, D_RWKV, t), F32),
                   jax.ShapeDtypeStruct((b, N_PAIR, HEAD_DIM, LANE), F32)],
        scratch_shapes=[pltpu.VMEM((nb, N_PAIR, HEAD_DIM, LANE), F32)],
        compiler_params=_cparams(("arbitrary", "arbitrary")),
        name="rwkv_scan",
    )(r, w, k, nkk, kka, vt, s0)


def _post_body(yt_ref, bonus_ref, gate_ref, lnw_ref, lnb_ref, bd_ref, o_ref, *, transposed):
    y = yt_ref[0].T if transposed else yt_ref[0]
    mean = _dot3_right(y, bd_ref[...])
    d = y - mean
    var = _dot3_right(d * d, bd_ref[...])
    yn = d * lax.rsqrt(var + GN_EPS) * lnw_ref[...] + lnb_ref[...]
    o_ref[0] = ((yn + bonus_ref[0]) * gate_ref[0]).astype(BF16)


def _post(yt, bonus, gate, lnw, lnb, bdm, tm, transposed):
    b, t, _ = bonus.shape
    full = lambda a: pl.BlockSpec(a.shape, lambda i, j: (0,) * a.ndim)
    blk = pl.BlockSpec((1, tm, D_RWKV), lambda i, j: (i, j, 0))
    return pl.pallas_call(
        functools.partial(_post_body, transposed=transposed),
        grid=(b, t // tm),
        in_specs=[pl.BlockSpec((1, D_RWKV, tm), lambda i, j: (i, 0, j)) if transposed else blk, blk, blk, full(lnw), full(lnb), full(bdm)],
        out_specs=blk,
        out_shape=jax.ShapeDtypeStruct((b, t, D_RWKV), BF16),
        compiler_params=_cparams(("arbitrary", "arbitrary")),
        name="rwkv_post",
    )(yt, bonus, gate, lnw, lnb, bdm)


def _cumsum_body(x_ref, tri_ref, cum_ref, cumt_ref, carry):
    @pl.when(pl.program_id(1) == 0)
    def _():
        carry[...] = jnp.zeros(carry.shape, F32)

    c = _dot3_left(tri_ref[...], x_ref[0]) + carry[...]
    cum_ref[0] = c
    cumt_ref[0] = c.T[0:8, :]
    n = c.shape[0]
    carry[...] = c[n - 1:n, :]


def _cumsum(lf, tri):
    b, t, _ = lf.shape
    tc = tri.shape[0]
    return pl.pallas_call(
        _cumsum_body,
        grid=(b, t // tc),
        in_specs=[pl.BlockSpec((1, tc, LANE), lambda i, j: (i, j, 0)),
                  pl.BlockSpec(tri.shape, lambda i, j: (0, 0))],
        out_specs=[pl.BlockSpec((1, tc, LANE), lambda i, j: (i, j, 0)),
                   pl.BlockSpec((1, 8, tc), lambda i, j: (i, 0, j))],
        out_shape=[jax.ShapeDtypeStruct((b, t, LANE), F32), jax.ShapeDtypeStruct((b, 8, t), F32)],
        scratch_shapes=[pltpu.VMEM((1, LANE), F32)],
        compiler_params=_cparams(("arbitrary", "arbitrary")),
        name="logf_cumsum",
    )(lf, tri)


def _k_window(h):
    return (h // 2) * LANE


def _v_window(h):
    return ((H_ATT + h) // 2) * LANE


def _valid_half(h, shape):
    lane = lax.broadcasted_iota(jnp.int32, shape, len(shape) - 1)
    return (lane >= HEAD_DIM) if (h + 1) % 2 else (lane < HEAD_DIM)


def _nt_dot(q, k):
    return lax.dot_general(q, k, (((1,), (1,)), ((), ())), preferred_element_type=F32)


def _fox_body(qt_ref, kt_ref, q_ref, kv_ref, cum_ref, cumt_ref, o_ref, m_scr, l_scr, acc_scr, *, tq, tk):
    p = pl.program_id(1)
    qi = qt_ref[p]
    ki = kt_ref[p]

    @pl.when(ki == 0)
    def _():
        m_scr[...] = jnp.full(m_scr.shape, -jnp.inf, F32)
        l_scr[...] = jnp.zeros(l_scr.shape, F32)
        acc_scr[...] = jnp.zeros(acc_scr.shape, F32)

    rowp = qi * tq + lax.broadcasted_iota(jnp.int32, (tq, tk), 0)
    colp = ki * tk + lax.broadcasted_iota(jnp.int32, (tq, tk), 1)
    mask = colp <= rowp
    for h in range(H_ATT):
        q = q_ref[0, :, h * LANE:(h + 1) * LANE]
        kw = kv_ref[0, :, _k_window(h):_k_window(h) + LANE]
        vw = kv_ref[0, :, _v_window(h):_v_window(h) + LANE]
        s = _nt_dot(q, kw) + cum_ref[0, :, h:h + 1] - cumt_ref[0, h:h + 1, :]
        s = jnp.where(mask, s, -jnp.inf)
        m_old = m_scr[h]
        m_new = jnp.maximum(m_old, jnp.max(s, axis=1, keepdims=True))
        alpha = jnp.exp(m_old - m_new)
        pr = jnp.exp(s - m_new)
        l_scr[h] = alpha * l_scr[h] + jnp.sum(pr, axis=1, keepdims=True)
        acc_scr[h] = alpha * acc_scr[h] + _dot(pr.astype(BF16), vw)
        m_scr[h] = m_new

    @pl.when(ki == qi)
    def _():
        for h in range(H_ATT):
            y = acc_scr[h] / l_scr[h]
            o_ref[0, :, h * LANE:(h + 1) * LANE] = jnp.where(_valid_half(h, y.shape), y, 0.0).astype(BF16)


def _sb_body(qt_ref, kt_ref, q_ref, kv_ref, u_ref, o_ref, c_scr, acc_scr, *, tq, tk):
    p = pl.program_id(1)
    qi = qt_ref[p]
    ki = kt_ref[p]

    @pl.when(ki == qi)
    def _():
        c_scr[...] = jnp.zeros(c_scr.shape, F32)
        acc_scr[...] = jnp.zeros(acc_scr.shape, F32)

    rowp = qi * tq + lax.broadcasted_iota(jnp.int32, (tq, tk), 0)
    colp = ki * tk + lax.broadcasted_iota(jnp.int32, (tq, tk), 1)
    mask = colp < rowp
    for h in range(H_ATT):
        q = q_ref[0, :, h * LANE:(h + 1) * LANE]
        kw = kv_ref[0, :, _k_window(h):_k_window(h) + LANE]
        vw = kv_ref[0, :, _v_window(h):_v_window(h) + LANE]
        z = _nt_dot(q, kw)
        x = jnp.where(mask, -_softplus(z), 0.0)
        suf = _dot2_right(x, u_ref[...])
        a = jnp.where(mask, jnp.exp(z + x + suf + c_scr[h]), 0.0)
        acc_scr[h] = acc_scr[h] + _dot(a.astype(BF16), vw)
        c_scr[h] = c_scr[h] + suf[:, 0:1] + x[:, 0:1]

    @pl.when(ki == 0)
    def _():
        for h in range(H_ATT):
            y = acc_scr[h]
            o_ref[0, :, h * LANE:(h + 1) * LANE] = jnp.where(_valid_half(h, y.shape), y, 0.0).astype(BF16)


def _tri_tables(nq, descending):
    qt, kt = [], []
    for qi in range(nq):
        ks = range(qi, -1, -1) if descending else range(qi + 1)
        for ki in ks:
            qt.append(qi)
            kt.append(ki)
    return jnp.asarray(qt, jnp.int32), jnp.asarray(kt, jnp.int32)


def _fox(q, kvb, cum, cumt, tq):
    b, t, _ = q.shape
    qt, kt = _tri_tables(t // tq, descending=False)
    grid_spec = pltpu.PrefetchScalarGridSpec(
        num_scalar_prefetch=2,
        grid=(b, int(qt.shape[0])),
        in_specs=[pl.BlockSpec((1, tq, QPAD), lambda i, p, qt, kt: (i, qt[p], 0)),
                  pl.BlockSpec((1, tq, 2 * D_ATT), lambda i, p, qt, kt: (i, kt[p], 0)),
                  pl.BlockSpec((1, tq, LANE), lambda i, p, qt, kt: (i, qt[p], 0)),
                  pl.BlockSpec((1, 8, tq), lambda i, p, qt, kt: (i, 0, kt[p]))],
        out_specs=pl.BlockSpec((1, tq, QPAD), lambda i, p, qt, kt: (i, qt[p], 0)),
        scratch_shapes=[pltpu.VMEM((H_ATT, tq, 1), F32), pltpu.VMEM((H_ATT, tq, 1), F32),
                        pltpu.VMEM((H_ATT, tq, LANE), F32)])
    return pl.pallas_call(
        functools.partial(_fox_body, tq=tq, tk=tq),
        grid_spec=grid_spec,
        out_shape=jax.ShapeDtypeStruct((b, t, QPAD), BF16),
        compiler_params=_cparams(("arbitrary", "arbitrary")),
        name="fox_attn",
    )(qt, kt, q, kvb, cum, cumt)


def _sb(q, kvb, u, tq):
    b, t, _ = q.shape
    qt, kt = _tri_tables(t // tq, descending=True)
    grid_spec = pltpu.PrefetchScalarGridSpec(
        num_scalar_prefetch=2,
        grid=(b, int(qt.shape[0])),
        in_specs=[pl.BlockSpec((1, tq, QPAD), lambda i, p, qt, kt: (i, qt[p], 0)),
                  pl.BlockSpec((1, tq, 2 * D_ATT), lambda i, p, qt, kt: (i, kt[p], 1)),
                  pl.BlockSpec(u.shape, lambda i, p, qt, kt: (0, 0))],
        out_specs=pl.BlockSpec((1, tq, QPAD), lambda i, p, qt, kt: (i, qt[p], 0)),
        scratch_shapes=[pltpu.VMEM((H_ATT, tq, 1), F32), pltpu.VMEM((H_ATT, tq, LANE), F32)])
    return pl.pallas_call(
        functools.partial(_sb_body, tq=tq, tk=tq),
        grid_spec=grid_spec,
        out_shape=jax.ShapeDtypeStruct((b, t, QPAD), BF16),
        compiler_params=_cparams(("arbitrary", "arbitrary")),
        name="sb_attn",
    )(qt, kt, q, kvb, u)


N_QROW = 32
V_FOX0 = 2 * LANE
V_SB0 = 7 * LANE
N_VWIN = 3
V_SPAN = N_VWIN * LANE


def _paged_body(pt_ref, *refs, pp, page):
    pages = refs[:pp]
    lfts = refs[pp:2 * pp]
    qall_ref, kvn_ref, lfn_ref, yb_ref, yc_ref = refs[2 * pp:2 * pp + 5]
    new_scr, m_scr, l_scr, accf_scr, accs_scr, cf_scr, cs_scr, cq_scr = refs[2 * pp + 5:]
    g = pl.program_id(1)
    n_new = kvn_ref.shape[1]

    ji = lax.broadcasted_iota(jnp.int32, (page, page), 0)
    si = lax.broadcasted_iota(jnp.int32, (page, page), 1)
    u_excl = (ji > si).astype(BF16)
    u_incl = (ji <= si).astype(BF16)
    key = lax.broadcasted_iota(jnp.int32, (N_QROW, page), 1)
    qrow = lax.broadcasted_iota(jnp.int32, (N_QROW, page), 0) // 8

    def attend(kv_bf, gneg, mask_f, mask_s):
        sc = _nt_dot(qall_ref[0], kv_bf)
        s = sc[:N_QROW] + cq_scr[...] + jnp.concatenate([gneg] * 4, axis=0)
        if mask_f is not None:
            s = jnp.where(mask_f, s, -jnp.inf)
        m_old = m_scr[...]
        m_new = jnp.maximum(m_old, jnp.max(s, axis=1, keepdims=True))
        alpha = jnp.exp(m_old - m_new)
        pr = jnp.exp(s - m_new)
        l_scr[...] = alpha * l_scr[...] + jnp.sum(pr, axis=1, keepdims=True)
        pv = _dot(pr.astype(BF16), kv_bf[:, V_FOX0:V_FOX0 + V_SPAN])
        for w in range(N_VWIN):
            accf_scr[w] = alpha * accf_scr[w] + pv[:, w * LANE:(w + 1) * LANE]
        m_scr[...] = m_new

        z = sc[N_QROW:]
        x = -_softplus(z)
        if mask_s is not None:
            x = jnp.where(mask_s, x, 0.0)
        suf = _dot3_right(x, u_excl)
        a = jnp.exp(z + x + suf + cs_scr[...])
        if mask_s is not None:
            a = jnp.where(mask_s, a, 0.0)
        av = _dot(a.astype(BF16), kv_bf[:, V_SB0:V_SB0 + V_SPAN])
        for w in range(N_VWIN):
            accs_scr[w] = accs_scr[w] + av[:, w * LANE:(w + 1) * LANE]
        cs_scr[...] = cs_scr[...] + suf[:, 0:1] + x[:, 0:1]

    @pl.when(g == 0)
    def _():
        m_scr[...] = jnp.full(m_scr.shape, -jnp.inf, F32)
        l_scr[...] = jnp.zeros(l_scr.shape, F32)
        accf_scr[...] = jnp.zeros(accf_scr.shape, F32)
        accs_scr[...] = jnp.zeros(accs_scr.shape, F32)
        cf_scr[...] = jnp.zeros(cf_scr.shape, F32)
        cs_scr[...] = jnp.zeros(cs_scr.shape, F32)
        new_scr[...] = jnp.zeros(new_scr.shape, F32)
        new_scr[0:n_new, :] = kvn_ref[0]
        pn = _dot3_right(lfn_ref[0], u_incl)
        for i in range(n_new):
            cq_scr[8 * i:8 * i + 8, :] = jnp.broadcast_to(pn[:, i:i + 1], (8, page))
        attend(new_scr[...].astype(BF16), -pn, key <= qrow, key < qrow)

    for i in range(pp):
        lf = lfts[i][0]
        suf = _dot3_right(lf, u_excl)
        attend(pages[i][0].astype(BF16), suf + cf_scr[...], None, None)
        cf_scr[...] = cf_scr[...] + suf[:, 0:1] + lf[:, 0:1]

    @pl.when(g == pl.num_programs(1) - 1)
    def _():
        inv_l = 1.0 / l_scr[...]
        for w in range(N_VWIN):
            accf_scr[w] = accf_scr[w] * inv_l
        for h in range(H_ATT):
            w = (_v_window(h) - V_FOX0) // LANE
            rows = pl.ds(h, n_new, stride=8)
            yf = accf_scr[w, rows, :]
            ys = accs_scr[w, rows, :]
            ok = _valid_half(h, yf.shape)
            yb_ref[0, :, h * LANE:(h + 1) * LANE] = jnp.where(ok, yf, 0.0)
            yc_ref[0, :, h * LANE:(h + 1) * LANE] = jnp.where(ok, ys, 0.0)


def _paged(page_table, cache_rows, cache_lft, qall, kv_new, lft_new, pp):
    db, n_pages = page_table.shape
    page = cache_rows.shape[1]
    n_new = kv_new.shape[1]
    ngrp = n_pages // pp

    def page_map(i):
        return lambda b, g, pt: (pt[b, n_pages - 1 - (g * pp + i)], 0, 0)

    per_seq = lambda a: pl.BlockSpec((1,) + a.shape[1:], lambda b, g, pt: (b,) + (0,) * (a.ndim - 1))
    grid_spec = pltpu.PrefetchScalarGridSpec(
        num_scalar_prefetch=1,
        grid=(db, ngrp),
        in_specs=([pl.BlockSpec((1, page, KV_W), page_map(i)) for i in range(pp)]
                  + [pl.BlockSpec((1, 8, page), page_map(i)) for i in range(pp)]
                  + [per_seq(qall), per_seq(kv_new), per_seq(lft_new)]),
        out_specs=[pl.BlockSpec((1, n_new, QPAD), lambda b, g, pt: (b, 0, 0))] * 2,
        scratch_shapes=[pltpu.VMEM((page, KV_W), F32),
                        pltpu.VMEM((N_QROW, 1), F32), pltpu.VMEM((N_QROW, 1), F32),
                        pltpu.VMEM((N_VWIN, N_QROW, LANE), F32), pltpu.VMEM((N_VWIN, N_QROW, LANE), F32),
                        pltpu.VMEM((8, 1), F32), pltpu.VMEM((N_QROW, 1), F32),
                        pltpu.VMEM((N_QROW, page), F32)])
    o = jax.ShapeDtypeStruct((db, n_new, QPAD), F32)
    return pl.pallas_call(
        functools.partial(_paged_body, pp=pp, page=page),
        grid_spec=grid_spec,
        out_shape=[o, o],
        compiler_params=_cparams(("arbitrary", "arbitrary")),
        name="paged_attn",
    )(page_table, *([cache_rows] * pp), *([cache_lft] * pp), qall, kv_new, lft_new)


def _ffn_body(x_ref, ya_ref, yb_ref, yc_ref, woa_ref, wob_ref, woc_ref, g_ref, wg_ref, wu_ref, wd_ref,
              o_ref, h_scr, hn_scr, acc_scr):
    j = pl.program_id(1)

    @pl.when(j == 0)
    def _():
        h = (x_ref[...] + _dot(ya_ref[...].astype(BF16), woa_ref[...])
             + _dot(yb_ref[...].astype(BF16), wob_ref[...]) + _dot(yc_ref[...].astype(BF16), woc_ref[...]))
        h_scr[...] = h
        ms = jnp.mean(h * h, axis=-1, keepdims=True)
        hn_scr[...] = (h * lax.rsqrt(ms + RMS_EPS) * g_ref[...]).astype(BF16)
        acc_scr[...] = jnp.zeros(acc_scr.shape, F32)

    hn = hn_scr[...]
    gate = _dot(hn, wg_ref[...])
    up = _dot(hn, wu_ref[...])
    act = (gate * _sigmoid(gate) * up).astype(BF16)
    acc_scr[...] += _dot(act, wd_ref[...])

    @pl.when(j == pl.num_programs(1) - 1)
    def _():
        o_ref[...] = h_scr[...] + acc_scr[...]


def _ffn(x, ya, yb, yc, woa, wob, woc, g2, wg, wu, wd, tm, tf):
    n = x.shape[0]
    row = lambda c: pl.BlockSpec((tm, c), lambda i, j: (i, 0))
    full = lambda a: pl.BlockSpec(a.shape, lambda i, j: (0,) * a.ndim)
    return pl.pallas_call(
        _ffn_body,
        grid=(n // tm, D_FF // tf),
        in_specs=[row(D_MODEL), row(D_RWKV), row(QPAD), row(QPAD), full(woa), full(wob), full(woc), full(g2),
                  pl.BlockSpec((D_MODEL, tf), lambda i, j: (0, j)),
                  pl.BlockSpec((D_MODEL, tf), lambda i, j: (0, j)),
                  pl.BlockSpec((tf, D_MODEL), lambda i, j: (j, 0))],
        out_specs=row(D_MODEL),
        out_shape=jax.ShapeDtypeStruct((n, D_MODEL), F32),
        scratch_shapes=[pltpu.VMEM((tm, D_MODEL), F32), pltpu.VMEM((tm, D_MODEL), BF16),
                        pltpu.VMEM((tm, D_MODEL), F32)],
        compiler_params=_cparams(("arbitrary", "arbitrary")),
        name="merge_ffn",
    )(x, ya, yb, yc, woa, wob, woc, g2, wg, wu, wd)


def _head_slots(w):
    w = w.reshape(w.shape[:-1] + (H_ATT, 1, HEAD_DIM))
    half = (np.arange(H_ATT) % 2)[:, None] == np.arange(2)[None, :]
    w = jnp.where(jnp.asarray(half)[:, :, None], w, 0.0)
    return w.reshape(w.shape[:-3] + (QPAD,))


def _out_rows(w):
    w = w.reshape(H_ATT, 1, HEAD_DIM, w.shape[-1])
    half = ((np.arange(H_ATT) + 1) % 2)[:, None] == np.arange(2)[None, :]
    w = jnp.where(jnp.asarray(half)[:, :, None, None], w, 0.0)
    return w.reshape(QPAD, w.shape[-1])


def _block_diag(n, blk, val, limit=None):
    i = np.arange(n)
    m = (i[:, None] // blk == i[None, :] // blk).astype(np.float32) * val
    if limit is not None:
        m = m * ((i[:, None] < limit) & (i[None, :] < limit))
    return jnp.asarray(m, BF16)


def _pair_state(s):
    b = s.shape[0]
    s = s.reshape(b, N_PAIR, 2, HEAD_DIM, HEAD_DIM)
    return jnp.swapaxes(s, 2, 3).reshape(b, N_PAIR, HEAD_DIM, LANE)


def _unpair_state(s):
    b = s.shape[0]
    s = s.reshape(b, N_PAIR, HEAD_DIM, 2, HEAD_DIM)
    return jnp.swapaxes(s, 2, 3).reshape(b, H_RWKV, HEAD_DIM, HEAD_DIM)


def _pick(n, prefs):
    for p in prefs:
        if n % p == 0:
            return p
    return n


def kernel(x_prompt, x_sample, cache_kv, cache_logf, state_wkv, state_shift, page_table,
           norm1_g, w_in, shift_mu, rwkv_w0, rwkv_w2, rwkv_a0, rwkv_a2, rwkv_k_k, rwkv_k_a,
           rwkv_r_k, rwkv_ln_w, rwkv_ln_b, fox_b_f, fox_q_g, fox_k_g, w_out, norm2_g,
           ffn_w_gate, ffn_w_up, ffn_w_down):
    depth = w_in.shape[0]
    bp, tp, _ = x_prompt.shape
    db, ts, _ = x_sample.shape
    n_phys, page = cache_kv.shape[1], cache_kv.shape[2]

    o = N_SHIFT
    w_fq, w_fk, w_fv = (w_in[:, :, o + i * D_ATT:o + (i + 1) * D_ATT] for i in range(3))
    w_f = w_in[:, :, o + 3 * D_ATT:o + N_FOX_COLS]
    o = N_SHIFT + N_FOX_COLS
    w_sq, w_sk, w_sv = (w_in[:, :, o + i * D_ATT:o + (i + 1) * D_ATT] for i in range(3))
    w_proj = jnp.concatenate(
        [w_in[:, :, :N_SHIFT], w_fk, w_fv, w_sk, w_sv, _head_slots(w_fq), _head_slots(w_sq),
         jnp.pad(w_f, ((0, 0), (0, 0), (0, LANE - H_ATT)))], axis=-1).astype(BF16)
    bf_pad = jnp.pad(fox_b_f, ((0, 0), (0, LANE - H_ATT)))[:, None, :]
    kg_row = jnp.pad(jnp.tile(fox_k_g, (1, H_ATT)), ((0, 0), (0, 3 * LANE - D_ATT)))[:, None, :]
    qg_row = jnp.tile(jnp.tile(fox_q_g, (1, 2)), (1, H_ATT))[:, None, :]
    bdk = _block_diag(3 * LANE, HEAD_DIM, 1.0 / HEAD_DIM, limit=D_ATT)
    bdq = _block_diag(QPAD, LANE, 1.0 / HEAD_DIM)
    bd_sum = _block_diag(D_RWKV, HEAD_DIM, 1.0)
    bd_mean = _block_diag(D_RWKV, HEAD_DIM, 1.0 / HEAD_DIM)
    zl = jnp.zeros((depth, D_LORA, D_RWKV), F32)
    w2p = jnp.concatenate([rwkv_w2, zl], axis=1).astype(BF16)
    a2p = jnp.concatenate([zl, rwkv_a2], axis=1).astype(BF16)
    row3 = lambda a: a.reshape(depth, 1, -1)
    wo_a = w_out[:, :D_RWKV].astype(BF16)
    wo_b = jax.vmap(_out_rows)(w_out[:, D_RWKV:D_RWKV + D_ATT]).astype(BF16)
    wo_c = jax.vmap(_out_rows)(w_out[:, D_RWKV + D_ATT:]).astype(BF16)
    wg, wu, wd = ffn_w_gate.astype(BF16), ffn_w_up.astype(BF16), ffn_w_down.astype(BF16)

    tq = _pick(tp, (256, 128))
    tc_sum = _pick(tp, (256, 128))
    tri = jnp.asarray(np.tril(np.ones((tc_sum, tc_sum), np.float32)), BF16)
    u_sb = jnp.asarray(np.tril(np.ones((tq, tq), np.float32), -1), BF16)

    cache_rows = cache_kv.reshape(depth, n_phys, page, KV_W)
    cache_lft = jnp.pad(jnp.swapaxes(cache_logf, 2, 3), ((0, 0), (0, 0), (0, 8 - H_ATT), (0, 0)))
    hh, ww = np.arange(8)[None, :, None], np.arange(KV_W // LANE)[None, None, :]
    slot_onehot = jnp.asarray((hh < H_ATT) & (ww == H_ATT * np.arange(2)[:, None, None] + hh // 2), BF16)

    np_rows, ns_rows = bp * tp, db * ts
    tm_p = _pick(np_rows, (256, 128))
    tm_s = _pick(ns_rows, (256, 128, 64, 32, 16, 8))
    tc_scan = _pick(tp, (128,))
    nb_s = _pick(db, (8, 4, 2, 1))
    pp = _pick(page_table.shape[1], (4, 2, 1))
    tf = _pick(D_FF, (256, 128))

    xp = x_prompt.reshape(np_rows, D_MODEL)
    xs = x_sample.reshape(ns_rows, D_MODEL)
    outs = [[] for _ in range(8)]

    def rwkv(feats, prev, s0p, l, nb, tc):
        nseq, t, _ = feats.shape
        shifted = jnp.concatenate([prev[:, None, :], feats[:, :-1]], axis=1)
        prep_w = (row3(shift_mu)[l], row3(rwkv_w0)[l], w2p[l], row3(rwkv_a0)[l], a2p[l],
                  row3(rwkv_k_k)[l], row3(rwkv_k_a)[l], row3(rwkv_r_k)[l], bd_sum)
        post_w = (row3(rwkv_ln_w)[l], row3(rwkv_ln_b)[l], bd_mean)
        if t % LANE == 0:
            tm = _pick(t, (256, 128))
            r, w, k, nkk, kka, vt, bonus, gate = _prep(feats, shifted, *prep_w, tm, True)
            yt, s_fin = _scan(r, w, k, nkk, kka, vt, s0p, nb, tc)
            ya = _post(yt, bonus, gate, *post_w, tm, True)
        else:
            rows = nseq * t
            tm = _pick(rows, (256, 128, 64, 32, 16, 8))
            flat = lambda a: a.reshape(1, rows, a.shape[-1])
            res = _prep(flat(feats), flat(shifted), *prep_w, tm, False)
            r, w, k, nkk, kka, v, bonus, gate = (a.reshape(nseq, t, D_RWKV) for a in res)
            yt, s_fin = _scan(r, w, k, nkk, kka, jnp.swapaxes(v, 1, 2), s0p, nb, tc)
            ya = _post(flat(jnp.swapaxes(yt, 1, 2)), flat(bonus), flat(gate), *post_w, tm, False)
        return ya.reshape(nseq * t, D_RWKV), s_fin

    for l in range(depth):
        proj_consts = (row3(norm1_g)[l], w_proj[l], bdk, bdq, kg_row[l], qg_row[l], bf_pad[l])
        ffn_consts = (wo_a[l], wo_b[l], wo_c[l], row3(norm2_g)[l], wg[l], wu[l], wd[l])

        feats, kv, kvb, qf, qs, lf = _proj(xp, *proj_consts, tm_p)
        feats3 = feats.reshape(bp, tp, N_SHIFT)
        ya, s_fin = rwkv(feats3, jnp.zeros((bp, N_SHIFT), F32),
                         jnp.zeros((bp, N_PAIR, HEAD_DIM, LANE), F32), l, bp, tc_scan)
        cum, cumt = _cumsum(lf.reshape(bp, tp, LANE), tri)
        kvb3 = kvb.reshape(bp, tp, KV_W)
        yb = _fox(qf.reshape(bp, tp, QPAD), kvb3, cum, cumt, tq)
        yc = _sb(qs.reshape(bp, tp, QPAD), kvb3, u_sb, tq)
        xp = _ffn(xp, ya, yb.reshape(np_rows, QPAD), yc.reshape(np_rows, QPAD), *ffn_consts, tm_p, tf)
        outs[0].append(kv.reshape(bp, tp, 4, H_ATT, HEAD_DIM))
        outs[1].append(lf[:, :H_ATT].reshape(bp, tp, H_ATT))
        outs[2].append(_unpair_state(s_fin))
        outs[3].append(feats3[:, -1])

        feats, kv, kvb, qf, qs, lf = _proj(xs, *proj_consts, tm_s)
        feats3 = feats.reshape(db, ts, N_SHIFT)
        ya, s_fin = rwkv(feats3, state_shift[l], _pair_state(state_wkv[l]), l, nb_s, ts)
        q5 = jnp.stack([qf.reshape(db, ts, H_ATT, LANE), qs.reshape(db, ts, H_ATT, LANE)], axis=1)
        q5 = jnp.pad(q5, ((0, 0), (0, 0), (0, 0), (0, 8 - H_ATT), (0, 0)))
        qall = q5[:, :, :, :, None, :] * slot_onehot[None, :, None, :, :, None]
        qall = qall.reshape(db, 2 * ts * 8, KV_W)
        lft_new = jnp.pad(jnp.swapaxes(lf.reshape(db, ts, LANE)[:, :, :8], 1, 2), ((0, 0), (0, 0), (0, page - ts)))
        yb, yc = _paged(page_table, cache_rows[l], cache_lft[l], qall, kv.reshape(db, ts, KV_W), lft_new, pp)
        xs = _ffn(xs, ya, yb.reshape(ns_rows, QPAD), yc.reshape(ns_rows, QPAD), *ffn_consts, tm_s, tf)
        outs[4].append(kv.reshape(db, ts, 4, H_ATT, HEAD_DIM))
        outs[5].append(lf[:, :H_ATT].reshape(db, ts, H_ATT))
        outs[6].append(_unpair_state(s_fin))
        outs[7].append(feats3[:, -1])

    return (xp.reshape(bp, tp, D_MODEL), xs.reshape(db, ts, D_MODEL)) + tuple(jnp.stack(o) for o in outs)
```

```python
import functools

import numpy as np
import jax
import jax.numpy as jnp
from jax import lax
from jax.experimental import pallas as pl
from jax.experimental.pallas import tpu as pltpu

F32 = jnp.float32
BF16 = jnp.bfloat16

D_MODEL = 1024
HEAD_DIM = 64
H_RWKV = 6
H_ATT = 5
D_RWKV = H_RWKV * HEAD_DIM
D_ATT = H_ATT * HEAD_DIM
D_LORA = 64
N_SHIFT = 4 * D_RWKV + 2 * D_LORA
N_FOX_COLS = 3 * D_ATT + H_ATT
D_FF = 2816
KV_W = 4 * D_ATT
RMS_EPS = 1e-6
GN_EPS = 64e-5
ATTN_SCALE = HEAD_DIM ** -0.5

LANE = 128
SUBLANE = 8
QPAD = H_ATT * LANE
N_PAIR = H_RWKV // 2
PAIR_ROWS = N_PAIR * HEAD_DIM

C_FEATS = 0
C_KV = N_SHIFT
C_QF = C_KV + KV_W
C_QS = C_QF + QPAD
C_F = C_QS + QPAD
N_PROJ = C_F + LANE

VMEM_LIMIT = 56 * 1024 * 1024


def _cparams(sem):
    return pltpu.CompilerParams(dimension_semantics=sem, vmem_limit_bytes=VMEM_LIMIT)


def _dot(a, b):
    return jnp.dot(a, b, preferred_element_type=F32)


def _nt_dot(a, b):
    return lax.dot_general(a, b, (((1,), (1,)), ((), ())), preferred_element_type=F32)


def _split3(x):
    hi = x.astype(BF16)
    r1 = x - hi.astype(F32)
    mid = r1.astype(BF16)
    lo = (r1 - mid.astype(F32)).astype(BF16)
    return hi, mid, lo


def _dot3_right(x, m):
    hi, mid, lo = _split3(x)
    return _dot(hi, m) + _dot(mid, m) + _dot(lo, m)


def _dot3_left(m, x):
    hi, mid, lo = _split3(x)
    return _dot(m, hi) + _dot(m, mid) + _dot(m, lo)


def _split2(x):
    hi = x.astype(BF16)
    return hi, (x - hi.astype(F32)).astype(BF16)


def _dot2_right(x, m):
    hi, lo = _split2(x)
    return _dot(hi, m) + _dot(lo, m)


def _softplus(x):
    return jnp.maximum(x, 0.0) + jnp.log1p(jnp.exp(-jnp.abs(x)))


def _sigmoid(x):
    return 1.0 / (1.0 + jnp.exp(-x))


def _proj_body(x_ref, g_ref, w_ref, bdk_ref, bdq_ref, kg_ref, qg_ref, bf_ref,
               feats_ref, kv_ref, kvb_ref, qf_ref, qs_ref, lf_ref):
    x = x_ref[...]
    ms = jnp.mean(x * x, axis=-1, keepdims=True)
    xn = (x * lax.rsqrt(ms + RMS_EPS) * g_ref[...]).astype(BF16)

    feats_ref[...] = _dot(xn, w_ref[:, C_FEATS:C_KV])

    ukv = _dot(xn, w_ref[:, C_KV:C_QF])
    ufk = ukv[:, :3 * LANE]
    msk = _dot3_right(ufk * ufk, bdk_ref[...])
    col = lax.broadcasted_iota(jnp.int32, ufk.shape, 1)
    kv0 = jnp.where(col < D_ATT, ufk * lax.rsqrt(msk + RMS_EPS) * kg_ref[...], ufk)
    kv_ref[:, :3 * LANE] = kv0
    kv_ref[:, 3 * LANE:] = ukv[:, 3 * LANE:]
    kvb_ref[:, :3 * LANE] = kv0.astype(BF16)
    kvb_ref[:, 3 * LANE:] = ukv[:, 3 * LANE:].astype(BF16)

    uqf = _dot(xn, w_ref[:, C_QF:C_QS])
    msq = _dot3_right(uqf * uqf, bdq_ref[...])
    qf_ref[...] = (uqf * lax.rsqrt(msq + RMS_EPS) * qg_ref[...] * ATTN_SCALE).astype(BF16)

    uqs = _dot(xn, w_ref[:, C_QS:C_F])
    qs_ref[...] = (uqs * ATTN_SCALE).astype(BF16)

    uf = _dot(xn, w_ref[:, C_F:N_PROJ]) + bf_ref[...]
    lf_ref[...] = -_softplus(-uf)


def _proj(x, g, w, bdk, bdq, kg, qg, bf, tm):
    n = x.shape[0]
    full = lambda a: pl.BlockSpec(a.shape, lambda i: (0,) * a.ndim)
    row = lambda c: pl.BlockSpec((tm, c), lambda i: (i, 0))
    return pl.pallas_call(
        _proj_body,
        grid=(n // tm,),
        in_specs=[row(D_MODEL), full(g), full(w), full(bdk), full(bdq), full(kg), full(qg), full(bf)],
        out_specs=[row(N_SHIFT), row(KV_W), row(KV_W), row(QPAD), row(QPAD), row(LANE)],
        out_shape=[jax.ShapeDtypeStruct((n, N_SHIFT), F32),
                   jax.ShapeDtypeStruct((n, KV_W), F32),
                   jax.ShapeDtypeStruct((n, KV_W), BF16),
                   jax.ShapeDtypeStruct((n, QPAD), BF16),
                   jax.ShapeDtypeStruct((n, QPAD), BF16),
                   jax.ShapeDtypeStruct((n, LANE), F32)],
        compiler_params=_cparams(("arbitrary",)),
        name="proj",
    )(x, g, w, bdk, bdq, kg, qg, bf)


def _prep_body(f_ref, sh_ref, mu_ref, w0_ref, w2_ref, a0_ref, a2_ref, kk_ref, ka_ref, rk_ref, bd_ref,
               nkk_ref, w_ref, kka_ref, k_ref, r_ref, v_ref, bonus_ref, gate_ref):
    f = f_ref[...]
    m = f + (sh_ref[...] - f) * mu_ref[...]
    r = m[:, 0:D_RWKV]
    k = m[:, D_RWKV:2 * D_RWKV]
    v = m[:, 2 * D_RWKV:3 * D_RWKV]
    g = m[:, 3 * D_RWKV:4 * D_RWKV]
    lo = m[:, 4 * D_RWKV:N_SHIFT]
    lw = _dot(jnp.tanh(lo).astype(BF16), w2_ref[...])
    la = _dot(lo.astype(BF16), a2_ref[...])
    w = -_softplus(-(w0_ref[...] + lw)) - 0.5
    decay = jnp.exp(-jnp.exp(w))
    a = _sigmoid(a0_ref[...] + la)
    kk = k * kk_ref[...]
    ss = _dot3_right(kk * kk, bd_ref[...])
    kk = kk / jnp.maximum(jnp.sqrt(ss), 1e-12)
    k2 = k * (1.0 + (a - 1.0) * ka_ref[...])
    rks = _dot3_right(r * k2 * rk_ref[...], bd_ref[...])
    nkk_ref[...] = -kk
    w_ref[...] = decay
    kka_ref[...] = kk * a
    k_ref[...] = k2
    r_ref[...] = r
    v_ref[...] = v
    bonus_ref[...] = rks * v
    gate_ref[...] = _sigmoid(g)


def _prep(feats, shifted, mu, w0, w2p, a0, a2p, k_k, k_a, r_k, bd, tm):
    n = feats.shape[0]
    full = lambda a: pl.BlockSpec(a.shape, lambda i: (0,) * a.ndim)
    blk = lambda c: pl.BlockSpec((tm, c), lambda i: (i, 0))
    o384 = jax.ShapeDtypeStruct((n, D_RWKV), F32)
    return pl.pallas_call(
        _prep_body,
        grid=(n // tm,),
        in_specs=[blk(N_SHIFT), blk(N_SHIFT)] + [full(a) for a in (mu, w0, w2p, a0, a2p, k_k, k_a, r_k, bd)],
        out_specs=[blk(D_RWKV)] * 8,
        out_shape=[o384] * 8,
        compiler_params=_cparams(("arbitrary",)),
        name="rwkv_prep",
    )(feats, shifted, mu, w0, w2p, a0, a2p, k_k, k_a, r_k, bd)


def _scan_body(nkk_ref, w_ref, kka_ref, k_ref, r_ref, v_ref, s0_ref, ones_ref, bv_ref,
               y_ref, sT_ref, s_scr, *, nb, tc):
    c = pl.program_id(1)

    @pl.when(c == 0)
    def _():
        s_scr[...] = s0_ref[...]

    ones2 = ones_ref[...]
    zpad = jnp.zeros((LANE - 3 * SUBLANE, LANE), F32)
    ztile = jnp.zeros((HEAD_DIM, LANE), F32)
    lane_in_half = lax.broadcasted_iota(jnp.int32, (PAIR_ROWS, LANE), 1) % HEAD_DIM

    def hilo(z):
        return jnp.concatenate(_split2(z), axis=1)

    def group(i, carry):
        t0 = pl.multiple_of(i * SUBLANE, SUBLANE)
        lhs = []
        for b in range(nb):
            vr = v_ref[b, pl.ds(t0, SUBLANE), :]
            for j in range(N_PAIR):
                hi, mid, lo = _split3(vr[:, j * LANE:(j + 1) * LANE])
                stacked = jnp.concatenate([hi.astype(F32), mid.astype(F32), lo.astype(F32), zpad], axis=0)
                at = stacked.T.astype(BF16)
                lhs.append(jnp.concatenate([at[:HEAD_DIM], at[HEAD_DIM:]], axis=1))
        vt_all = _dot(jnp.concatenate(lhs, axis=0), bv_ref[...])
        def bc(x, u):
            return jnp.concatenate(
                [jnp.broadcast_to(x[u:u + 1, j * LANE:(j + 1) * LANE], (HEAD_DIM, LANE))
                 for j in range(N_PAIR)], axis=0)

        rows = [[ref[b, pl.ds(t0, SUBLANE), :] for ref in (nkk_ref, w_ref, kka_ref, k_ref, r_ref)]
                for b in range(nb)]
        s = [s_scr[b] for b in range(nb)]
        g_acc = [jnp.zeros((PAIR_ROWS, LANE), F32) for _ in range(nb)]
        sa = [_dot(hilo(s[b] * bc(rows[b][0], 0)), ones2) for b in range(nb)]
        for u in range(SUBLANE):
            for b in range(nb):
                nkk, w, kka, k, r = rows[b]
                vt = vt_all[b * PAIR_ROWS:(b + 1) * PAIR_ROWS, u * LANE:(u + 1) * LANE]
                s[b] = s[b] * bc(w, u) + sa[b] * bc(kka, u) + vt * bc(k, u)
                zy = hilo(s[b] * bc(r, u))
                if u + 1 < SUBLANE:
                    res = _dot(jnp.concatenate([hilo(s[b] * bc(nkk, u + 1)), zy], axis=0), ones2)
                    sa[b], yt = res[:PAIR_ROWS], res[PAIR_ROWS:]
                else:
                    yt = _dot(zy, ones2)
                g_acc[b] = jnp.where(lane_in_half == u, yt, g_acc[b])
        for b in range(nb):
            s_scr[b] = s[b]
            for j in range(N_PAIR):
                gj = g_acc[b][j * HEAD_DIM:(j + 1) * HEAD_DIM]
                tt = jnp.concatenate([gj, ztile], axis=0).T
                y_ref[b, pl.ds(t0, SUBLANE), j * LANE:(j + 1) * LANE] = (
                    tt[:SUBLANE] + pltpu.roll(tt[HEAD_DIM:HEAD_DIM + SUBLANE], HEAD_DIM, 1))
        return carry

    lax.fori_loop(0, tc // SUBLANE, group, 0)

    @pl.when(c == pl.num_programs(1) - 1)
    def _():
        sT_ref[...] = s_scr[...]


def _scan_consts():
    k = np.arange(2 * LANE)
    seg_k = (k % LANE) // HEAD_DIM
    lane = np.arange(LANE)
    ones2 = (seg_k[:, None] == lane[None, :] // HEAD_DIM)
    cc = k % LANE
    col = np.arange(SUBLANE * LANE)
    bv = ((cc[:, None] < 3 * SUBLANE) & (cc[:, None] % SUBLANE == col[None, :] // LANE)
          & ((k[:, None] // LANE) == (col[None, :] % LANE) // HEAD_DIM))
    as_bf = lambda a: jnp.asarray(a.astype(np.float32), BF16)
    return as_bf(ones2), as_bf(bv)


def _scan(nkk, w, kka, k, r, v, s0, nb, tc):
    b, t, _ = r.shape
    ones2, bv = _scan_consts()
    blk = pl.BlockSpec((nb, tc, D_RWKV), lambda i, c: (i, c, 0))
    sblk = pl.BlockSpec((nb, PAIR_ROWS, LANE), lambda i, c: (i, 0, 0))
    full = lambda a: pl.BlockSpec(a.shape, lambda i, c: (0,) * a.ndim)
    return pl.pallas_call(
        functools.partial(_scan_body, nb=nb, tc=tc),
        grid=(b // nb, t // tc),
        in_specs=[blk] * 6 + [sblk, full(ones2), full(bv)],
        out_specs=[blk, sblk],
        out_shape=[jax.ShapeDtypeStruct((b, t, D_RWKV), F32),
                   jax.ShapeDtypeStruct((b, PAIR_ROWS, LANE), F32)],
        scratch_shapes=[pltpu.VMEM((nb, PAIR_ROWS, LANE), F32)],
        compiler_params=_cparams(("arbitrary", "arbitrary")),
        name="rwkv_scan",
    )(nkk, w, kka, k, r, v, s0, ones2, bv)


def _post_body(y_ref, bonus_ref, gate_ref, lnw_ref, lnb_ref, bd_ref, o_ref):
    y = y_ref[...]
    mean = _dot3_right(y, bd_ref[...])
    d = y - mean
    var = _dot3_right(d * d, bd_ref[...])
    yn = d * lax.rsqrt(var + GN_EPS) * lnw_ref[...] + lnb_ref[...]
    o_ref[...] = ((yn + bonus_ref[...]) * gate_ref[...]).astype(BF16)


def _post(y, bonus, gate, lnw, lnb, bdm, tm):
    n = y.shape[0]
    full = lambda a: pl.BlockSpec(a.shape, lambda i: (0,) * a.ndim)
    blk = pl.BlockSpec((tm, D_RWKV), lambda i: (i, 0))
    return pl.pallas_call(
        _post_body,
        grid=(n // tm,),
        in_specs=[blk, blk, blk, full(lnw), full(lnb), full(bdm)],
        out_specs=blk,
        out_shape=jax.ShapeDtypeStruct((n, D_RWKV), BF16),
        compiler_params=_cparams(("arbitrary",)),
        name="rwkv_post",
    )(y, bonus, gate, lnw, lnb, bdm)


def _cumsum_body(x_ref, tri_ref, cum_ref, cumt_ref, carry):
    @pl.when(pl.program_id(1) == 0)
    def _():
        carry[...] = jnp.zeros(carry.shape, F32)

    c = _dot3_left(tri_ref[...], x_ref[0]) + carry[...]
    cum_ref[0] = c
    cumt_ref[0] = c.T[0:SUBLANE, :]
    n = c.shape[0]
    carry[...] = c[n - 1:n, :]


def _cumsum(lf, tri):
    b, t, _ = lf.shape
    tc = tri.shape[0]
    return pl.pallas_call(
        _cumsum_body,
        grid=(b, t // tc),
        in_specs=[pl.BlockSpec((1, tc, LANE), lambda i, j: (i, j, 0)),
                  pl.BlockSpec(tri.shape, lambda i, j: (0, 0))],
        out_specs=[pl.BlockSpec((1, tc, LANE), lambda i, j: (i, j, 0)),
                   pl.BlockSpec((1, SUBLANE, tc), lambda i, j: (i, 0, j))],
        out_shape=[jax.ShapeDtypeStruct((b, t, LANE), F32), jax.ShapeDtypeStruct((b, SUBLANE, t), F32)],
        scratch_shapes=[pltpu.VMEM((1, LANE), F32)],
        compiler_params=_cparams(("arbitrary", "arbitrary")),
        name="logf_cumsum",
    )(lf, tri)


N_SPLIT = 2


def _fox_body(q_ref, k_ref, v_ref, cum_ref, cumt_ref, o_ref, *, tq):
    h = pl.program_id(1)
    qi = pl.program_id(2)
    th = tq // N_SPLIT
    lane = lax.broadcasted_iota(jnp.int32, (tq, LANE), 1)
    cq_all = jnp.sum(jnp.where(lane == h, cum_ref[0], 0.0), axis=1, keepdims=True)
    sub = lax.broadcasted_iota(jnp.int32, (SUBLANE, tq), 0)
    qs = [q_ref[0, a * th:(a + 1) * th, :] for a in range(N_SPLIT)]
    cqs = [cq_all[a * th:(a + 1) * th] for a in range(N_SPLIT)]
    rowp = lax.broadcasted_iota(jnp.int32, (th, tq), 0)
    colp = lax.broadcasted_iota(jnp.int32, (th, tq), 1)

    def block(j, carry, diagonal):
        k0 = pl.multiple_of(j * tq, tq)
        kw = k_ref[0, pl.ds(k0, tq), :]
        vw = v_ref[0, pl.ds(k0, tq), :]
        ck = jnp.sum(jnp.where(sub == h, cumt_ref[0, :, pl.ds(k0, tq)], 0.0), axis=0, keepdims=True)
        out = []
        for a in range(N_SPLIT):
            m_old, l_old, acc = carry[a]
            s = _nt_dot(qs[a], kw) + cqs[a] - ck
            if diagonal:
                s = jnp.where(colp <= rowp + a * th, s, -jnp.inf)
            m_new = jnp.maximum(m_old, jnp.max(s, axis=1, keepdims=True))
            alpha = jnp.exp(m_old - m_new)
            pr = jnp.exp(s - m_new)
            l_new = alpha * l_old + jnp.sum(pr, axis=1, keepdims=True)
            acc = alpha * acc + _dot(pr.astype(BF16), vw)
            out.append((m_new, l_new, acc))
        return tuple(out)

    init = tuple((jnp.full((th, 1), -jnp.inf, F32), jnp.zeros((th, 1), F32), jnp.zeros((th, LANE), F32))
                 for _ in range(N_SPLIT))
    carry = lax.fori_loop(0, qi, lambda j, c: block(j, c, False), init)
    carry = block(qi, carry, True)
    lane_o = lax.broadcasted_iota(jnp.int32, (th, LANE), 1)
    valid = (lane_o >= HEAD_DIM) == ((h + 1) % 2 == 1)
    for a in range(N_SPLIT):
        _, l_fin, acc = carry[a]
        o_ref[0, a * th:(a + 1) * th, :] = jnp.where(valid, acc / l_fin, 0.0).astype(BF16)


def _sb_body(q_ref, k_ref, v_ref, u_ref, o_ref, *, tq):
    h = pl.program_id(1)
    qi = pl.program_id(2)
    th = tq // N_SPLIT
    sk = u_ref.shape[1]
    qs = [q_ref[0, a * th:(a + 1) * th, :] for a in range(N_SPLIT)]
    rowp = lax.broadcasted_iota(jnp.int32, (th, tq), 0)
    colp = lax.broadcasted_iota(jnp.int32, (th, tq), 1)

    def block(j, carry, diagonal):
        k0 = pl.multiple_of(j * tq, tq)
        kw = k_ref[0, pl.ds(k0, tq), :]
        vw = v_ref[0, pl.ds(k0, tq), :]
        out = []
        for a in range(N_SPLIT):
            c_old, acc = carry[a]
            z = _nt_dot(qs[a], kw)
            x = jnp.minimum(-z, 0.0) - jnp.log(1.0 + jnp.exp(-jnp.abs(z)))
            zx = x + z
            if diagonal:
                mask = colp < rowp + a * th
                x = jnp.where(mask, x, 0.0)
            c_run = c_old
            ws = []
            for sb in reversed(range(tq // sk)):
                xs = x[:, sb * sk:(sb + 1) * sk]
                suf = _dot(jnp.concatenate(_split2(xs), axis=1), u_ref[...]) + c_run
                ws.append(jnp.exp(zx[:, sb * sk:(sb + 1) * sk] + suf))
                c_run = suf[:, 0:1] + xs[:, 0:1]
            w = jnp.concatenate(ws[::-1], axis=1)
            if diagonal:
                w = jnp.where(mask, w, 0.0)
            out.append((c_run, acc + _dot(w.astype(BF16), vw)))
        return tuple(out)

    init = tuple((jnp.zeros((th, 1), F32), jnp.zeros((th, LANE), F32)) for _ in range(N_SPLIT))
    carry = block(qi, init, True)
    carry = lax.fori_loop(0, qi, lambda i, c: block(qi - 1 - i, c, False), carry)
    lane_o = lax.broadcasted_iota(jnp.int32, (th, LANE), 1)
    valid = (lane_o >= HEAD_DIM) == ((h + 1) % 2 == 1)
    for a in range(N_SPLIT):
        o_ref[0, a * th:(a + 1) * th, :] = jnp.where(valid, carry[a][1], 0.0).astype(BF16)


def _attn_specs(tq, t, group):
    g0 = group * (2 * D_ATT // LANE)
    return ([pl.BlockSpec((1, tq, LANE), lambda b, h, i: (b, i, h)),
             pl.BlockSpec((1, t, LANE), lambda b, h, i: (b, 0, g0 + h // 2)),
             pl.BlockSpec((1, t, LANE), lambda b, h, i: (b, 0, g0 + (H_ATT + h) // 2))],
            pl.BlockSpec((1, tq, LANE), lambda b, h, i: (b, i, h)))


def _fox(q, kvb, cum, cumt, tq):
    b, t, _ = q.shape
    ins, out = _attn_specs(tq, t, 0)
    return pl.pallas_call(
        functools.partial(_fox_body, tq=tq),
        grid=(b, H_ATT, t // tq),
        in_specs=ins + [pl.BlockSpec((1, tq, LANE), lambda b, h, i: (b, i, 0)),
                        pl.BlockSpec((1, SUBLANE, t), lambda b, h, i: (b, 0, 0))],
        out_specs=out,
        out_shape=jax.ShapeDtypeStruct((b, t, QPAD), BF16),
        compiler_params=_cparams(("arbitrary", "arbitrary", "arbitrary")),
        name="fox_attn",
    )(q, kvb, kvb, cum, cumt)


def _sb(q, kvb, u, tq):
    b, t, _ = q.shape
    ins, out = _attn_specs(tq, t, 1)
    return pl.pallas_call(
        functools.partial(_sb_body, tq=tq),
        grid=(b, H_ATT, t // tq),
        in_specs=ins + [pl.BlockSpec(u.shape, lambda b, h, i: (0, 0))],
        out_specs=out,
        out_shape=jax.ShapeDtypeStruct((b, t, QPAD), BF16),
        compiler_params=_cparams(("arbitrary", "arbitrary", "arbitrary")),
        name="sb_attn",
    )(q, kvb, kvb, u)


def _paged_body(pt_ref, *refs, pp, page, n_new):
    pages = refs[:pp]
    lfts = refs[pp:2 * pp]
    qf_ref, qs_ref, kvn_ref, lfn_ref, yb_ref, yc_ref = refs[2 * pp:2 * pp + 6]
    m_scr, l_scr, accf_scr, accs_scr, cf_scr, cs_scr, cq_scr = refs[2 * pp + 6:]
    g = pl.program_id(1)
    nrow = SUBLANE * n_new

    ji = lax.broadcasted_iota(jnp.int32, (page, page), 0)
    si = lax.broadcasted_iota(jnp.int32, (page, page), 1)
    u_excl = (ji > si).astype(BF16)
    u_incl = (ji <= si).astype(BF16)
    key = lax.broadcasted_iota(jnp.int32, (nrow, page), 1)
    qrow = lax.broadcasted_iota(jnp.int32, (nrow, page), 0) // SUBLANE

    def attend(blocks, biases, mask_f, mask_s):
        n = len(blocks)
        sf = [_dot(qf_ref[0], blk[0]) + cq_scr[...] + jnp.concatenate([bias] * n_new, axis=0)
              for blk, bias in zip(blocks, biases)]
        if mask_f is not None:
            sf = [jnp.where(mask_f, s, -jnp.inf) for s in sf]
        m_old = m_scr[...]
        m_new = jnp.maximum(m_old, jnp.max(functools.reduce(jnp.maximum, sf), axis=1, keepdims=True))
        alpha = jnp.exp(m_old - m_new)
        prs = [jnp.exp(s - m_new) for s in sf]
        l_scr[...] = alpha * l_scr[...] + jnp.sum(functools.reduce(jnp.add, prs), axis=1, keepdims=True)
        pv = functools.reduce(jnp.add, [_nt_dot(p.astype(BF16), blk[1]) for p, blk in zip(prs, blocks)])
        accf_scr[...] = alpha * accf_scr[...] + pv
        m_scr[...] = m_new

        zs = [_dot(qs_ref[0], blk[2]) for blk in blocks]
        xs = [-_softplus(z) for z in zs]
        if mask_s is not None:
            xs = [jnp.where(mask_s, x, 0.0) for x in xs]
        sufs = [_dot3_right(x, u_excl) for x in xs]
        c = cs_scr[...]
        av = None
        for z, x, suf, blk in zip(zs, xs, sufs, blocks):
            a = jnp.exp(z + x + suf + c)
            if mask_s is not None:
                a = jnp.where(mask_s, a, 0.0)
            d = _nt_dot(a.astype(BF16), blk[3])
            av = d if av is None else av + d
            c = c + suf[:, 0:1] + x[:, 0:1]
        accs_scr[...] = accs_scr[...] + av
        cs_scr[...] = c

    @pl.when(g == 0)
    def _():
        m_scr[...] = jnp.full(m_scr.shape, -jnp.inf, F32)
        l_scr[...] = jnp.zeros(l_scr.shape, F32)
        accf_scr[...] = jnp.zeros(accf_scr.shape, F32)
        accs_scr[...] = jnp.zeros(accs_scr.shape, F32)
        cf_scr[...] = jnp.zeros(cf_scr.shape, F32)
        cs_scr[...] = jnp.zeros(cs_scr.shape, F32)
        pn = _dot3_right(lfn_ref[0], u_incl)
        for i in range(n_new):
            cq_scr[SUBLANE * i:SUBLANE * (i + 1), :] = jnp.broadcast_to(pn[:, i:i + 1], (SUBLANE, page))
        attend([tuple(kvn_ref[0, sl] for sl in range(4))], [-pn], key <= qrow, key < qrow)

    blocks, biases = [], []
    cf = cf_scr[...]
    for i in range(pp):
        blocks.append(tuple(pages[i][0, sl].reshape(D_ATT, page).astype(BF16) for sl in range(4)))
        lf = lfts[i][0]
        suf = _dot3_right(lf, u_excl)
        biases.append(suf + cf)
        cf = cf + suf[:, 0:1] + lf[:, 0:1]
    cf_scr[...] = cf
    attend(blocks, biases, None, None)

    @pl.when(g == pl.num_programs(1) - 1)
    def _():
        hrow = lax.broadcasted_iota(jnp.int32, (SUBLANE, D_ATT), 0)
        hcol = lax.broadcasted_iota(jnp.int32, (SUBLANE, D_ATT), 1) // HEAD_DIM
        own = hrow == hcol
        inv_l = 1.0 / l_scr[...]
        for i in range(n_new):
            rs = slice(SUBLANE * i, SUBLANE * (i + 1))
            yf = accf_scr[rs, :] * inv_l[rs]
            yb_ref[0, i:i + 1, :] = jnp.sum(jnp.where(own, yf, 0.0), axis=0, keepdims=True)
            yc_ref[0, i:i + 1, :] = jnp.sum(jnp.where(own, accs_scr[rs, :], 0.0), axis=0, keepdims=True)


def _paged(page_table, cache_t, layer, cache_lft, qbd_f, qbd_s, kvn_t, lft_new, pp):
    db, n_pages = page_table.shape
    page = cache_t.shape[-1]
    n_new = qbd_f.shape[1] // SUBLANE
    nrow = SUBLANE * n_new
    ngrp = n_pages // pp

    def page_map(i, nd):
        return lambda b, g, pt: (layer, pt[b, n_pages - 1 - (g * pp + i)]) + (0,) * nd

    per_seq = lambda a: pl.BlockSpec((1,) + a.shape[1:], lambda b, g, pt: (b,) + (0,) * (a.ndim - 1))
    grid_spec = pltpu.PrefetchScalarGridSpec(
        num_scalar_prefetch=1,
        grid=(db, ngrp),
        in_specs=([pl.BlockSpec((1, 1, 4, H_ATT, HEAD_DIM, page), page_map(i, 4)) for i in range(pp)]
                  + [pl.BlockSpec((1, 1, SUBLANE, page), page_map(i, 2)) for i in range(pp)]
                  + [per_seq(qbd_f), per_seq(qbd_s), per_seq(kvn_t), per_seq(lft_new)]),
        out_specs=[pl.BlockSpec((1, n_new, D_ATT), lambda b, g, pt: (b, 0, 0))] * 2,
        scratch_shapes=[pltpu.VMEM((nrow, 1), F32), pltpu.VMEM((nrow, 1), F32),
                        pltpu.VMEM((nrow, D_ATT), F32), pltpu.VMEM((nrow, D_ATT), F32),
                        pltpu.VMEM((SUBLANE, 1), F32), pltpu.VMEM((nrow, 1), F32),
                        pltpu.VMEM((nrow, page), F32)])
    o = jax.ShapeDtypeStruct((db, n_new, D_ATT), F32)

    def body(pt_ref, *refs):
        pages = [r.at[0] for r in refs[:pp]]
        lfts = [r.at[0] for r in refs[pp:2 * pp]]
        _paged_body(pt_ref, *pages, *lfts, *refs[2 * pp:], pp=pp, page=page, n_new=n_new)

    return pl.pallas_call(
        body,
        grid_spec=grid_spec,
        out_shape=[o, o],
        compiler_params=_cparams(("arbitrary", "arbitrary")),
        name="paged_attn",
    )(page_table, *([cache_t] * pp), *([cache_lft] * pp), qbd_f, qbd_s, kvn_t, lft_new)


def _ffn_body(x_ref, ya_ref, yb_ref, yc_ref, woa_ref, wob_ref, woc_ref, g_ref, wg_ref, wu_ref, wd_ref,
              o_ref, h_scr, hn_scr, acc_scr):
    j = pl.program_id(1)

    @pl.when(j == 0)
    def _():
        h = (x_ref[...] + _dot(ya_ref[...].astype(BF16), woa_ref[...])
             + _dot(yb_ref[...].astype(BF16), wob_ref[...]) + _dot(yc_ref[...].astype(BF16), woc_ref[...]))
        h_scr[...] = h
        ms = jnp.mean(h * h, axis=-1, keepdims=True)
        hn_scr[...] = (h * lax.rsqrt(ms + RMS_EPS) * g_ref[...]).astype(BF16)
        acc_scr[...] = jnp.zeros(acc_scr.shape, F32)

    hn = hn_scr[...]
    gate = _dot(hn, wg_ref[...])
    up = _dot(hn, wu_ref[...])
    act = (gate * _sigmoid(gate) * up).astype(BF16)
    acc_scr[...] += _dot(act, wd_ref[...])

    @pl.when(j == pl.num_programs(1) - 1)
    def _():
        o_ref[...] = h_scr[...] + acc_scr[...]


def _ffn(x, ya, yb, yc, woa, wob, woc, g2, wg, wu, wd, tm, tf):
    n = x.shape[0]
    row = lambda a: pl.BlockSpec((tm, a.shape[1]), lambda i, j: (i, 0))
    full = lambda a: pl.BlockSpec(a.shape, lambda i, j: (0,) * a.ndim)
    return pl.pallas_call(
        _ffn_body,
        grid=(n // tm, D_FF // tf),
        in_specs=[row(x), row(ya), row(yb), row(yc), full(woa), full(wob), full(woc), full(g2),
                  pl.BlockSpec((D_MODEL, tf), lambda i, j: (0, j)),
                  pl.BlockSpec((D_MODEL, tf), lambda i, j: (0, j)),
                  pl.BlockSpec((tf, D_MODEL), lambda i, j: (j, 0))],
        out_specs=row(x),
        out_shape=jax.ShapeDtypeStruct((n, D_MODEL), F32),
        scratch_shapes=[pltpu.VMEM((tm, D_MODEL), F32), pltpu.VMEM((tm, D_MODEL), BF16),
                        pltpu.VMEM((tm, D_MODEL), F32)],
        compiler_params=_cparams(("arbitrary", "arbitrary")),
        name="merge_ffn",
    )(x, ya, yb, yc, woa, wob, woc, g2, wg, wu, wd)


def _head_slots(w):
    w = w.reshape(w.shape[:-1] + (H_ATT, 1, HEAD_DIM))
    half = (np.arange(H_ATT) % 2)[:, None] == np.arange(2)[None, :]
    w = jnp.where(jnp.asarray(half)[:, :, None], w, 0.0)
    return w.reshape(w.shape[:-3] + (QPAD,))


def _dense_heads(q):
    q = q.reshape(q.shape[:-1] + (H_ATT, 2, HEAD_DIM))
    half = (np.arange(H_ATT) % 2)[:, None] == np.arange(2)[None, :]
    return jnp.sum(jnp.where(jnp.asarray(half)[:, :, None], q, 0), axis=-2)


def _out_rows(w):
    w = w.reshape(H_ATT, 1, HEAD_DIM, w.shape[-1])
    half = ((np.arange(H_ATT) + 1) % 2)[:, None] == np.arange(2)[None, :]
    w = jnp.where(jnp.asarray(half)[:, :, None, None], w, 0.0)
    return w.reshape(QPAD, w.shape[-1])


def _block_diag(n, blk, val, limit=None):
    i = np.arange(n)
    m = (i[:, None] // blk == i[None, :] // blk).astype(np.float32) * val
    if limit is not None:
        m = m * ((i[:, None] < limit) & (i[None, :] < limit))
    return jnp.asarray(m, BF16)


def _pair_state(s):
    b = s.shape[0]
    s = s.reshape(b, N_PAIR, 2, HEAD_DIM, HEAD_DIM)
    return jnp.swapaxes(s, 2, 3).reshape(b, PAIR_ROWS, LANE)


def _unpair_state(s):
    b = s.shape[0]
    s = s.reshape(b, N_PAIR, HEAD_DIM, 2, HEAD_DIM)
    return jnp.swapaxes(s, 2, 3).reshape(b, H_RWKV, HEAD_DIM, HEAD_DIM)


def _pick(n, prefs):
    for p in prefs:
        if n % p == 0:
            return p
    return n


def kernel(x_prompt, x_sample, cache_kv, cache_logf, state_wkv, state_shift, page_table,
           norm1_g, w_in, shift_mu, rwkv_w0, rwkv_w2, rwkv_a0, rwkv_a2, rwkv_k_k, rwkv_k_a,
           rwkv_r_k, rwkv_ln_w, rwkv_ln_b, fox_b_f, fox_q_g, fox_k_g, w_out, norm2_g,
           ffn_w_gate, ffn_w_up, ffn_w_down):
    depth = w_in.shape[0]
    bp, tp, _ = x_prompt.shape
    db, ts, _ = x_sample.shape
    page = cache_kv.shape[2]

    o = N_SHIFT
    w_fq, w_fk, w_fv = (w_in[:, :, o + i * D_ATT:o + (i + 1) * D_ATT] for i in range(3))
    w_f = w_in[:, :, o + 3 * D_ATT:o + N_FOX_COLS]
    o = N_SHIFT + N_FOX_COLS
    w_sq, w_sk, w_sv = (w_in[:, :, o + i * D_ATT:o + (i + 1) * D_ATT] for i in range(3))
    w_proj = jnp.concatenate(
        [w_in[:, :, :N_SHIFT], w_fk, w_fv, w_sk, w_sv, _head_slots(w_fq), _head_slots(w_sq),
         jnp.pad(w_f, ((0, 0), (0, 0), (0, LANE - H_ATT)))], axis=-1).astype(BF16)
    bf_pad = jnp.pad(fox_b_f, ((0, 0), (0, LANE - H_ATT)))[:, None, :]
    kg_row = jnp.pad(jnp.tile(fox_k_g, (1, H_ATT)), ((0, 0), (0, 3 * LANE - D_ATT)))[:, None, :]
    qg_row = jnp.tile(jnp.tile(fox_q_g, (1, 2)), (1, H_ATT))[:, None, :]
    bdk = _block_diag(3 * LANE, HEAD_DIM, 1.0 / HEAD_DIM, limit=D_ATT)
    bdq = _block_diag(QPAD, LANE, 1.0 / HEAD_DIM)
    bd_sum = _block_diag(D_RWKV, HEAD_DIM, 1.0)
    bd_mean = _block_diag(D_RWKV, HEAD_DIM, 1.0 / HEAD_DIM)
    zl = jnp.zeros((depth, D_LORA, D_RWKV), F32)
    w2p = jnp.concatenate([rwkv_w2, zl], axis=1).astype(BF16)
    a2p = jnp.concatenate([zl, rwkv_a2], axis=1).astype(BF16)
    row3 = lambda a: a.reshape(depth, 1, -1)
    wo_a = w_out[:, :D_RWKV].astype(BF16)
    wo_b = w_out[:, D_RWKV:D_RWKV + D_ATT].astype(BF16)
    wo_c = w_out[:, D_RWKV + D_ATT:].astype(BF16)
    wo_b_slots, wo_c_slots = jax.vmap(_out_rows)(wo_b), jax.vmap(_out_rows)(wo_c)
    wg, wu, wd = ffn_w_gate.astype(BF16), ffn_w_up.astype(BF16), ffn_w_down.astype(BF16)

    tq = _pick(tp, (512, 256, 128))
    tc_sum = _pick(tp, (256, 128))
    tri = jnp.asarray(np.tril(np.ones((tc_sum, tc_sum), np.float32)), BF16)
    sk = _pick(tq, (256, 128))
    u_sb = jnp.asarray(np.tile(np.tril(np.ones((sk, sk), np.float32), -1), (2, 1)), BF16)

    cache_t = jnp.transpose(cache_kv, (0, 1, 3, 4, 5, 2))
    cache_lft = jnp.pad(jnp.swapaxes(cache_logf, 2, 3), ((0, 0), (0, 0), (0, SUBLANE - H_ATT), (0, 0)))
    eye_h = jnp.asarray(np.eye(SUBLANE, H_ATT, dtype=np.float32), BF16)

    np_rows, ns_rows = bp * tp, db * ts
    tm_p = _pick(np_rows, (256, 128))
    tm_s = _pick(ns_rows, (256, 128, 64, 32, 16, 8))
    tc_scan = _pick(tp, (256, 128, 64, 32, 16, 8))
    nb_s = _pick(db, (4, 2, 1))
    pp = _pick(page_table.shape[1], (8, 4, 2, 1))
    tf = _pick(D_FF, (256, 128))
    ts_pad = -(-ts // SUBLANE) * SUBLANE

    xp = x_prompt.reshape(np_rows, D_MODEL)
    xs = x_sample.reshape(ns_rows, D_MODEL)
    outs = [[] for _ in range(8)]

    def rwkv(feats, prev, s0p, l, nb, tc, tm):
        nseq, t, _ = feats.shape
        rows = nseq * t
        shifted = jnp.concatenate([prev[:, None, :], feats[:, :-1]], axis=1)
        res = _prep(feats.reshape(rows, N_SHIFT), shifted.reshape(rows, N_SHIFT),
                    row3(shift_mu)[l], row3(rwkv_w0)[l], w2p[l], row3(rwkv_a0)[l], a2p[l],
                    row3(rwkv_k_k)[l], row3(rwkv_k_a)[l], row3(rwkv_r_k)[l], bd_sum, tm)
        seq = [a.reshape(nseq, t, D_RWKV) for a in res[:6]]
        t_pad = -(-t // SUBLANE) * SUBLANE
        if t_pad != t:
            pad = lambda a, v: jnp.pad(a, ((0, 0), (0, t_pad - t), (0, 0)), constant_values=v)
            seq = [pad(a, 1.0 if i == 1 else 0.0) for i, a in enumerate(seq)]
        y, s_fin = _scan(*seq, s0p, nb, min(tc, t_pad))
        y = y[:, :t].reshape(rows, D_RWKV)
        ya = _post(y, res[6], res[7], row3(rwkv_ln_w)[l], row3(rwkv_ln_b)[l], bd_mean, tm)
        return ya, s_fin

    for l in range(depth):
        proj_consts = (row3(norm1_g)[l], w_proj[l], bdk, bdq, kg_row[l], qg_row[l], bf_pad[l])
        ffn_tail = (row3(norm2_g)[l], wg[l], wu[l], wd[l])

        feats, kv, kvb, qf, qs, lf = _proj(xp, *proj_consts, tm_p)
        feats3 = feats.reshape(bp, tp, N_SHIFT)
        ya, s_fin = rwkv(feats3, jnp.zeros((bp, N_SHIFT), F32), jnp.zeros((bp, PAIR_ROWS, LANE), F32),
                         l, bp, tc_scan, tm_p)
        cum, cumt = _cumsum(lf.reshape(bp, tp, LANE), tri)
        kvb3 = kvb.reshape(bp, tp, KV_W)
        yb = _fox(qf.reshape(bp, tp, QPAD), kvb3, cum, cumt, tq)
        yc = _sb(qs.reshape(bp, tp, QPAD), kvb3, u_sb, tq)
        xp = _ffn(xp, ya, yb.reshape(np_rows, QPAD), yc.reshape(np_rows, QPAD),
                  wo_a[l], wo_b_slots[l], wo_c_slots[l], *ffn_tail, tm_p, tf)
        outs[0].append(kv.reshape(bp, tp, 4, H_ATT, HEAD_DIM))
        outs[1].append(lf[:, :H_ATT].reshape(bp, tp, H_ATT))
        outs[2].append(_unpair_state(s_fin))
        outs[3].append(feats3[:, -1])

        feats, kv, kvb, qf, qs, lf = _proj(xs, *proj_consts, tm_s)
        feats3 = feats.reshape(db, ts, N_SHIFT)
        ya, s_fin = rwkv(feats3, state_shift[l], _pair_state(state_wkv[l]), l, nb_s, ts_pad, tm_s)

        def score_rows(q):
            q = _dense_heads(q.reshape(db, ts, QPAD))
            q = q[:, :, None, :, :] * eye_h[None, None, :, :, None]
            return q.reshape(db, ts * SUBLANE, D_ATT)

        kvn_t = jnp.pad(jnp.transpose(kvb.reshape(db, ts, 4, D_ATT), (0, 2, 3, 1)),
                        ((0, 0), (0, 0), (0, 0), (0, page - ts)))
        lft_new = jnp.pad(jnp.swapaxes(lf.reshape(db, ts, LANE)[:, :, :SUBLANE], 1, 2),
                          ((0, 0), (0, 0), (0, page - ts)))
        yb, yc = _paged(page_table, cache_t, l, cache_lft, score_rows(qf), score_rows(qs), kvn_t, lft_new, pp)
        xs = _ffn(xs, ya, yb.reshape(ns_rows, D_ATT), yc.reshape(ns_rows, D_ATT),
                  wo_a[l], wo_b[l], wo_c[l], *ffn_tail, tm_s, tf)
        outs[4].append(kv.reshape(db, ts, 4, H_ATT, HEAD_DIM))
        outs[5].append(lf[:, :H_ATT].reshape(db, ts, H_ATT))
        outs[6].append(_unpair_state(s_fin))
        outs[7].append(feats3[:, -1])

    return (xp.reshape(bp, tp, D_MODEL), xs.reshape(db, ts, D_MODEL)) + tuple(jnp.stack(o) for o in outs)
```

```python
import functools

import numpy as np
import jax
import jax.numpy as jnp
from jax import lax
from jax.experimental import pallas as pl
from jax.experimental.pallas import tpu as pltpu

F32 = jnp.float32
BF16 = jnp.bfloat16

D_MODEL = 1024
HEAD_DIM = 64
H_RWKV = 6
H_ATT = 5
D_RWKV = H_RWKV * HEAD_DIM
D_ATT = H_ATT * HEAD_DIM
D_LORA = 64
N_SHIFT = 4 * D_RWKV + 2 * D_LORA
N_FOX_COLS = 3 * D_ATT + H_ATT
D_FF = 2816
KV_W = 4 * D_ATT
RMS_EPS = 1e-6
GN_EPS = 64e-5
ATTN_SCALE = HEAD_DIM ** -0.5

LANE = 128
SUBLANE = 8
QPAD = H_ATT * LANE
N_PAIR = H_RWKV // 2
PAIR_ROWS = N_PAIR * HEAD_DIM

C_FEATS = 0
C_KV = N_SHIFT
C_QF = C_KV + KV_W
C_QS = C_QF + QPAD
C_F = C_QS + QPAD
N_PROJ = C_F + LANE

VMEM_LIMIT = 56 * 1024 * 1024


def _cparams(sem):
    return pltpu.CompilerParams(dimension_semantics=sem, vmem_limit_bytes=VMEM_LIMIT)


def _dot(a, b):
    return jnp.dot(a, b, preferred_element_type=F32)


def _nt_dot(a, b):
    return lax.dot_general(a, b, (((1,), (1,)), ((), ())), preferred_element_type=F32)


def _split3(x):
    hi = x.astype(BF16)
    r1 = x - hi.astype(F32)
    mid = r1.astype(BF16)
    lo = (r1 - mid.astype(F32)).astype(BF16)
    return hi, mid, lo


def _dot3_right(x, m):
    hi, mid, lo = _split3(x)
    return _dot(hi, m) + _dot(mid, m) + _dot(lo, m)


def _dot3_left(m, x):
    hi, mid, lo = _split3(x)
    return _dot(m, hi) + _dot(m, mid) + _dot(m, lo)


def _split2(x):
    hi = x.astype(BF16)
    return hi, (x - hi.astype(F32)).astype(BF16)


def _dot2_right(x, m):
    hi, lo = _split2(x)
    return _dot(hi, m) + _dot(lo, m)


def _softplus(x):
    return jnp.maximum(x, 0.0) + jnp.log1p(jnp.exp(-jnp.abs(x)))


def _sigmoid(x):
    return 1.0 / (1.0 + jnp.exp(-x))


def _proj_body(x_ref, g_ref, w_ref, bdk_ref, bdq_ref, kg_ref, qg_ref, bf_ref,
               feats_ref, kv_ref, kvb_ref, qf_ref, qs_ref, lf_ref):
    x = x_ref[...]
    ms = jnp.mean(x * x, axis=-1, keepdims=True)
    xn = (x * lax.rsqrt(ms + RMS_EPS) * g_ref[...]).astype(BF16)

    feats_ref[...] = _dot(xn, w_ref[:, C_FEATS:C_KV])

    ukv = _dot(xn, w_ref[:, C_KV:C_QF])
    ufk = ukv[:, :3 * LANE]
    msk = _dot3_right(ufk * ufk, bdk_ref[...])
    col = lax.broadcasted_iota(jnp.int32, ufk.shape, 1)
    kv0 = jnp.where(col < D_ATT, ufk * lax.rsqrt(msk + RMS_EPS) * kg_ref[...], ufk)
    kv_ref[:, :3 * LANE] = kv0
    kv_ref[:, 3 * LANE:] = ukv[:, 3 * LANE:]
    kvb_ref[:, :3 * LANE] = kv0.astype(BF16)
    kvb_ref[:, 3 * LANE:] = ukv[:, 3 * LANE:].astype(BF16)

    uqf = _dot(xn, w_ref[:, C_QF:C_QS])
    msq = _dot3_right(uqf * uqf, bdq_ref[...])
    qf_ref[...] = (uqf * lax.rsqrt(msq + RMS_EPS) * qg_ref[...] * ATTN_SCALE).astype(BF16)

    uqs = _dot(xn, w_ref[:, C_QS:C_F])
    qs_ref[...] = (uqs * ATTN_SCALE).astype(BF16)

    uf = _dot(xn, w_ref[:, C_F:N_PROJ]) + bf_ref[...]
    lf_ref[...] = -_softplus(-uf)


def _proj(x, g, w, bdk, bdq, kg, qg, bf, tm):
    n = x.shape[0]
    full = lambda a: pl.BlockSpec(a.shape, lambda i: (0,) * a.ndim)
    row = lambda c: pl.BlockSpec((tm, c), lambda i: (i, 0))
    return pl.pallas_call(
        _proj_body,
        grid=(n // tm,),
        in_specs=[row(D_MODEL), full(g), full(w), full(bdk), full(bdq), full(kg), full(qg), full(bf)],
        out_specs=[row(N_SHIFT), row(KV_W), row(KV_W), row(QPAD), row(QPAD), row(LANE)],
        out_shape=[jax.ShapeDtypeStruct((n, N_SHIFT), F32),
                   jax.ShapeDtypeStruct((n, KV_W), F32),
                   jax.ShapeDtypeStruct((n, KV_W), BF16),
                   jax.ShapeDtypeStruct((n, QPAD), BF16),
                   jax.ShapeDtypeStruct((n, QPAD), BF16),
                   jax.ShapeDtypeStruct((n, LANE), F32)],
        compiler_params=_cparams(("arbitrary",)),
        name="proj",
    )(x, g, w, bdk, bdq, kg, qg, bf)


def _prep_body(f_ref, sh_ref, mu_ref, w0_ref, w2_ref, a0_ref, a2_ref, kk_ref, ka_ref, rk_ref, bd_ref,
               nkk_ref, w_ref, kka_ref, k_ref, r_ref, v_ref, bonus_ref, gate_ref):
    f = f_ref[...]
    m = f + (sh_ref[...] - f) * mu_ref[...]
    r = m[:, 0:D_RWKV]
    k = m[:, D_RWKV:2 * D_RWKV]
    v = m[:, 2 * D_RWKV:3 * D_RWKV]
    g = m[:, 3 * D_RWKV:4 * D_RWKV]
    lo = m[:, 4 * D_RWKV:N_SHIFT]
    lw = _dot(jnp.tanh(lo).astype(BF16), w2_ref[...])
    la = _dot(lo.astype(BF16), a2_ref[...])
    w = -_softplus(-(w0_ref[...] + lw)) - 0.5
    decay = jnp.exp(-jnp.exp(w))
    a = _sigmoid(a0_ref[...] + la)
    kk = k * kk_ref[...]
    ss = _dot3_right(kk * kk, bd_ref[...])
    kk = kk / jnp.maximum(jnp.sqrt(ss), 1e-12)
    k2 = k * (1.0 + (a - 1.0) * ka_ref[...])
    rks = _dot3_right(r * k2 * rk_ref[...], bd_ref[...])
    nkk_ref[...] = -kk
    w_ref[...] = decay
    kka_ref[...] = kk * a
    k_ref[...] = k2
    r_ref[...] = r
    v_ref[...] = v
    bonus_ref[...] = rks * v
    gate_ref[...] = _sigmoid(g)


def _prep(feats, shifted, mu, w0, w2p, a0, a2p, k_k, k_a, r_k, bd, tm):
    n = feats.shape[0]
    full = lambda a: pl.BlockSpec(a.shape, lambda i: (0,) * a.ndim)
    blk = lambda c: pl.BlockSpec((tm, c), lambda i: (i, 0))
    o384 = jax.ShapeDtypeStruct((n, D_RWKV), F32)
    return pl.pallas_call(
        _prep_body,
        grid=(n // tm,),
        in_specs=[blk(N_SHIFT), blk(N_SHIFT)] + [full(a) for a in (mu, w0, w2p, a0, a2p, k_k, k_a, r_k, bd)],
        out_specs=[blk(D_RWKV)] * 8,
        out_shape=[o384] * 8,
        compiler_params=_cparams(("arbitrary",)),
        name="rwkv_prep",
    )(feats, shifted, mu, w0, w2p, a0, a2p, k_k, k_a, r_k, bd)


def _scan_body(nkk_ref, w_ref, kka_ref, k_ref, r_ref, v_ref, s0_ref, ones_ref, bv_ref,
               y_ref, sT_ref, s_scr, *, nb, tc):
    c = pl.program_id(1)

    @pl.when(c == 0)
    def _():
        s_scr[...] = s0_ref[...]

    ones2 = ones_ref[...]
    zpad = jnp.zeros((LANE - 3 * SUBLANE, LANE), F32)
    ztile = jnp.zeros((HEAD_DIM, LANE), F32)
    lane_id = lax.broadcasted_iota(jnp.int32, (PAIR_ROWS, LANE), 1)
    lane_in_half = lane_id % HEAD_DIM
    left = lane_id < HEAD_DIM

    def hilo(z):
        return jnp.concatenate(_split2(z), axis=1)

    def group(i, carry):
        t0 = pl.multiple_of(i * SUBLANE, SUBLANE)
        lhs = []
        for b in range(nb):
            vr = v_ref[b, pl.ds(t0, SUBLANE), :]
            for j in range(N_PAIR):
                hi, mid, lo = _split3(vr[:, j * LANE:(j + 1) * LANE])
                stacked = jnp.concatenate([hi.astype(F32), mid.astype(F32), lo.astype(F32), zpad], axis=0)
                at = stacked.T.astype(BF16)
                lhs.append(jnp.concatenate([at[:HEAD_DIM], at[HEAD_DIM:]], axis=1))
        vt_all = _dot(jnp.concatenate(lhs, axis=0), bv_ref[...])
        def bc(x, u):
            return jnp.concatenate(
                [jnp.broadcast_to(x[u:u + 1, j * LANE:(j + 1) * LANE], (HEAD_DIM, LANE))
                 for j in range(N_PAIR)], axis=0)

        rows = [[ref[b, pl.ds(t0, SUBLANE), :] for ref in (nkk_ref, w_ref, kka_ref, k_ref, r_ref)]
                for b in range(nb)]
        s = [s_scr[b] for b in range(nb)]
        g_acc = [jnp.zeros((PAIR_ROWS, LANE), F32) for _ in range(nb)]
        for u in range(SUBLANE):
            for b in range(nb):
                nkk, w, kka, k, r = rows[b]
                z = s[b] * bc(nkk, u)
                sa = jnp.where(left,
                               jnp.sum(jnp.where(left, z, 0.0), axis=1, keepdims=True),
                               jnp.sum(jnp.where(left, 0.0, z), axis=1, keepdims=True))
                vt = vt_all[b * PAIR_ROWS:(b + 1) * PAIR_ROWS, u * LANE:(u + 1) * LANE]
                s[b] = s[b] * bc(w, u) + sa * bc(kka, u) + vt * bc(k, u)
                yt = _dot(hilo(s[b] * bc(r, u)), ones2)
                g_acc[b] = jnp.where(lane_in_half == u, yt, g_acc[b])
        for b in range(nb):
            s_scr[b] = s[b]
            for j in range(N_PAIR):
                gj = g_acc[b][j * HEAD_DIM:(j + 1) * HEAD_DIM]
                tt = jnp.concatenate([gj, ztile], axis=0).T
                y_ref[b, pl.ds(t0, SUBLANE), j * LANE:(j + 1) * LANE] = (
                    tt[:SUBLANE] + pltpu.roll(tt[HEAD_DIM:HEAD_DIM + SUBLANE], HEAD_DIM, 1))
        return carry

    lax.fori_loop(0, tc // SUBLANE, group, 0)

    @pl.when(c == pl.num_programs(1) - 1)
    def _():
        sT_ref[...] = s_scr[...]


def _scan_consts():
    k = np.arange(2 * LANE)
    seg_k = (k % LANE) // HEAD_DIM
    lane = np.arange(LANE)
    ones2 = (seg_k[:, None] == lane[None, :] // HEAD_DIM)
    cc = k % LANE
    col = np.arange(SUBLANE * LANE)
    bv = ((cc[:, None] < 3 * SUBLANE) & (cc[:, None] % SUBLANE == col[None, :] // LANE)
          & ((k[:, None] // LANE) == (col[None, :] % LANE) // HEAD_DIM))
    as_bf = lambda a: jnp.asarray(a.astype(np.float32), BF16)
    return as_bf(ones2), as_bf(bv)


def _scan(nkk, w, kka, k, r, v, s0, nb, tc):
    b, t, _ = r.shape
    ones2, bv = _scan_consts()
    blk = pl.BlockSpec((nb, tc, D_RWKV), lambda i, c: (i, c, 0))
    sblk = pl.BlockSpec((nb, PAIR_ROWS, LANE), lambda i, c: (i, 0, 0))
    full = lambda a: pl.BlockSpec(a.shape, lambda i, c: (0,) * a.ndim)
    return pl.pallas_call(
        functools.partial(_scan_body, nb=nb, tc=tc),
        grid=(b // nb, t // tc),
        in_specs=[blk] * 6 + [sblk, full(ones2), full(bv)],
        out_specs=[blk, sblk],
        out_shape=[jax.ShapeDtypeStruct((b, t, D_RWKV), F32),
                   jax.ShapeDtypeStruct((b, PAIR_ROWS, LANE), F32)],
        scratch_shapes=[pltpu.VMEM((nb, PAIR_ROWS, LANE), F32)],
        compiler_params=_cparams(("arbitrary", "arbitrary")),
        name="rwkv_scan",
    )(nkk, w, kka, k, r, v, s0, ones2, bv)


def _post_body(y_ref, bonus_ref, gate_ref, lnw_ref, lnb_ref, bd_ref, o_ref):
    y = y_ref[...]
    mean = _dot3_right(y, bd_ref[...])
    d = y - mean
    var = _dot3_right(d * d, bd_ref[...])
    yn = d * lax.rsqrt(var + GN_EPS) * lnw_ref[...] + lnb_ref[...]
    o_ref[...] = ((yn + bonus_ref[...]) * gate_ref[...]).astype(BF16)


def _post(y, bonus, gate, lnw, lnb, bdm, tm):
    n = y.shape[0]
    full = lambda a: pl.BlockSpec(a.shape, lambda i: (0,) * a.ndim)
    blk = pl.BlockSpec((tm, D_RWKV), lambda i: (i, 0))
    return pl.pallas_call(
        _post_body,
        grid=(n // tm,),
        in_specs=[blk, blk, blk, full(lnw), full(lnb), full(bdm)],
        out_specs=blk,
        out_shape=jax.ShapeDtypeStruct((n, D_RWKV), BF16),
        compiler_params=_cparams(("arbitrary",)),
        name="rwkv_post",
    )(y, bonus, gate, lnw, lnb, bdm)


def _cumsum_body(x_ref, tri_ref, cum_ref, cumt_ref, carry):
    @pl.when(pl.program_id(1) == 0)
    def _():
        carry[...] = jnp.zeros(carry.shape, F32)

    c = _dot3_left(tri_ref[...], x_ref[0]) + carry[...]
    cum_ref[0] = c
    cumt_ref[0] = c.T[0:SUBLANE, :]
    n = c.shape[0]
    carry[...] = c[n - 1:n, :]


def _cumsum(lf, tri):
    b, t, _ = lf.shape
    tc = tri.shape[0]
    return pl.pallas_call(
        _cumsum_body,
        grid=(b, t // tc),
        in_specs=[pl.BlockSpec((1, tc, LANE), lambda i, j: (i, j, 0)),
                  pl.BlockSpec(tri.shape, lambda i, j: (0, 0))],
        out_specs=[pl.BlockSpec((1, tc, LANE), lambda i, j: (i, j, 0)),
                   pl.BlockSpec((1, SUBLANE, tc), lambda i, j: (i, 0, j))],
        out_shape=[jax.ShapeDtypeStruct((b, t, LANE), F32), jax.ShapeDtypeStruct((b, SUBLANE, t), F32)],
        scratch_shapes=[pltpu.VMEM((1, LANE), F32)],
        compiler_params=_cparams(("arbitrary", "arbitrary")),
        name="logf_cumsum",
    )(lf, tri)


def _fox_body(q_ref, k_ref, v_ref, cum_ref, cumt_ref, o_ref, *, tq, tk):
    h = pl.program_id(1)
    qi = pl.program_id(2)
    nd = tq // tk
    lane = lax.broadcasted_iota(jnp.int32, (tq, LANE), 1)
    cq = jnp.sum(jnp.where(lane == h, cum_ref[0], 0.0), axis=1, keepdims=True)
    sub = lax.broadcasted_iota(jnp.int32, (SUBLANE, tk), 0)
    q = q_ref[0]
    rowp = qi * tq + lax.broadcasted_iota(jnp.int32, (tq, tk), 0)
    colp = lax.broadcasted_iota(jnp.int32, (tq, tk), 1)

    def block(j, carry, masked):
        k0 = pl.multiple_of(j * tk, tk)
        kw = k_ref[0, pl.ds(k0, tk), :]
        vw = v_ref[0, pl.ds(k0, tk), :]
        ck = jnp.sum(jnp.where(sub == h, cumt_ref[0, :, pl.ds(k0, tk)], 0.0), axis=0, keepdims=True)
        m_old, l_old, acc = carry
        s = _nt_dot(q, kw) + cq - ck
        if masked:
            s = jnp.where(colp + k0 <= rowp, s, -jnp.inf)
        m_new = jnp.maximum(m_old, jnp.max(s, axis=1, keepdims=True))
        alpha = jnp.exp(m_old - m_new)
        pr = jnp.exp(s - m_new)
        l_new = alpha * l_old + jnp.sum(pr, axis=1, keepdims=True)
        return m_new, l_new, alpha * acc + _dot(pr.astype(BF16), vw)

    carry = (jnp.full((tq, 1), -jnp.inf, F32), jnp.zeros((tq, 1), F32), jnp.zeros((tq, LANE), F32))
    carry = lax.fori_loop(0, qi * nd, lambda j, c: block(j, c, False), carry)
    for jj in range(nd):
        carry = block(qi * nd + jj, carry, True)
    lane_o = lax.broadcasted_iota(jnp.int32, (tq, LANE), 1)
    valid = (lane_o >= HEAD_DIM) == ((h + 1) % 2 == 1)
    o_ref[0] = jnp.where(valid, carry[2] / carry[1], 0.0).astype(BF16)


def _sb_body(q_ref, k_ref, v_ref, u_ref, o_ref, *, tq, tk):
    h = pl.program_id(1)
    qi = pl.program_id(2)
    nd = tq // tk
    sk = u_ref.shape[1]
    q = q_ref[0]
    rowp = qi * tq + lax.broadcasted_iota(jnp.int32, (tq, tk), 0)
    colp = lax.broadcasted_iota(jnp.int32, (tq, tk), 1)

    def block(j, carry, masked):
        k0 = pl.multiple_of(j * tk, tk)
        kw = k_ref[0, pl.ds(k0, tk), :]
        vw = v_ref[0, pl.ds(k0, tk), :]
        c_run, acc = carry
        z = _nt_dot(q, kw)
        x = jnp.minimum(-z, 0.0) - jnp.log(1.0 + jnp.exp(-jnp.abs(z)))
        zx = x + z
        if masked:
            mask = colp + k0 < rowp
            x = jnp.where(mask, x, 0.0)
        ws = []
        for sb in reversed(range(tk // sk)):
            xs = x[:, sb * sk:(sb + 1) * sk]
            suf = _dot(jnp.concatenate(_split2(xs), axis=1), u_ref[...]) + c_run
            ws.append(jnp.exp(zx[:, sb * sk:(sb + 1) * sk] + suf))
            c_run = suf[:, 0:1] + xs[:, 0:1]
        w = jnp.concatenate(ws[::-1], axis=1)
        if masked:
            w = jnp.where(mask, w, 0.0)
        return c_run, acc + _dot(w.astype(BF16), vw)

    carry = (jnp.zeros((tq, 1), F32), jnp.zeros((tq, LANE), F32))
    for jj in reversed(range(nd)):
        carry = block(qi * nd + jj, carry, True)
    carry = lax.fori_loop(0, qi * nd, lambda i, c: block(qi * nd - 1 - i, c, False), carry)
    lane_o = lax.broadcasted_iota(jnp.int32, (tq, LANE), 1)
    valid = (lane_o >= HEAD_DIM) == ((h + 1) % 2 == 1)
    o_ref[0] = jnp.where(valid, carry[1], 0.0).astype(BF16)


def _attn_specs(tq, t, group):
    g0 = group * (2 * D_ATT // LANE)
    return ([pl.BlockSpec((1, tq, LANE), lambda b, h, i: (b, i, h)),
             pl.BlockSpec((1, t, LANE), lambda b, h, i: (b, 0, g0 + h // 2)),
             pl.BlockSpec((1, t, LANE), lambda b, h, i: (b, 0, g0 + (H_ATT + h) // 2))],
            pl.BlockSpec((1, tq, LANE), lambda b, h, i: (b, i, h)))


def _fox(q, kvb, cum, cumt, tq, tk):
    b, t, _ = q.shape
    ins, out = _attn_specs(tq, t, 0)
    return pl.pallas_call(
        functools.partial(_fox_body, tq=tq, tk=tk),
        grid=(b, H_ATT, t // tq),
        in_specs=ins + [pl.BlockSpec((1, tq, LANE), lambda b, h, i: (b, i, 0)),
                        pl.BlockSpec((1, SUBLANE, t), lambda b, h, i: (b, 0, 0))],
        out_specs=out,
        out_shape=jax.ShapeDtypeStruct((b, t, QPAD), BF16),
        compiler_params=_cparams(("arbitrary", "arbitrary", "arbitrary")),
        name="fox_attn",
    )(q, kvb, kvb, cum, cumt)


def _sb(q, kvb, u, tq, tk):
    b, t, _ = q.shape
    ins, out = _attn_specs(tq, t, 1)
    return pl.pallas_call(
        functools.partial(_sb_body, tq=tq, tk=tk),
        grid=(b, H_ATT, t // tq),
        in_specs=ins + [pl.BlockSpec(u.shape, lambda b, h, i: (0, 0))],
        out_specs=out,
        out_shape=jax.ShapeDtypeStruct((b, t, QPAD), BF16),
        compiler_params=_cparams(("arbitrary", "arbitrary", "arbitrary")),
        name="sb_attn",
    )(q, kvb, kvb, u)


def _paged_body(pt_ref, *refs, pp, page, n_new):
    pages = refs[:pp]
    lfts = refs[pp:2 * pp]
    qf_ref, qs_ref, kvn_ref, lfn_ref, yb_ref, yc_ref = refs[2 * pp:2 * pp + 6]
    m_scr, l_scr, accf_scr, accs_scr, cf_scr, cs_scr, cq_scr = refs[2 * pp + 6:]
    g = pl.program_id(1)
    nrow = SUBLANE * n_new

    ji = lax.broadcasted_iota(jnp.int32, (page, page), 0)
    si = lax.broadcasted_iota(jnp.int32, (page, page), 1)
    u_excl = (ji > si).astype(BF16)
    u_incl = (ji <= si).astype(BF16)
    key = lax.broadcasted_iota(jnp.int32, (nrow, page), 1)
    qrow = lax.broadcasted_iota(jnp.int32, (nrow, page), 0) // SUBLANE

    def attend(blocks, biases, mask_f, mask_s):
        n = len(blocks)
        sf = [_dot(qf_ref[0], blk[0]) + cq_scr[...] + jnp.concatenate([bias] * n_new, axis=0)
              for blk, bias in zip(blocks, biases)]
        if mask_f is not None:
            sf = [jnp.where(mask_f, s, -jnp.inf) for s in sf]
        m_old = m_scr[...]
        m_new = jnp.maximum(m_old, jnp.max(functools.reduce(jnp.maximum, sf), axis=1, keepdims=True))
        alpha = jnp.exp(m_old - m_new)
        prs = [jnp.exp(s - m_new) for s in sf]
        l_scr[...] = alpha * l_scr[...] + jnp.sum(functools.reduce(jnp.add, prs), axis=1, keepdims=True)
        pv = functools.reduce(jnp.add, [_nt_dot(p.astype(BF16), blk[1]) for p, blk in zip(prs, blocks)])
        accf_scr[...] = alpha * accf_scr[...] + pv
        m_scr[...] = m_new

        zs = [_dot(qs_ref[0], blk[2]) for blk in blocks]
        xs = [-_softplus(z) for z in zs]
        if mask_s is not None:
            xs = [jnp.where(mask_s, x, 0.0) for x in xs]
        sufs = [_dot3_right(x, u_excl) for x in xs]
        c = cs_scr[...]
        av = None
        for z, x, suf, blk in zip(zs, xs, sufs, blocks):
            a = jnp.exp(z + x + suf + c)
            if mask_s is not None:
                a = jnp.where(mask_s, a, 0.0)
            d = _nt_dot(a.astype(BF16), blk[3])
            av = d if av is None else av + d
            c = c + suf[:, 0:1] + x[:, 0:1]
        accs_scr[...] = accs_scr[...] + av
        cs_scr[...] = c

    @pl.when(g == 0)
    def _():
        m_scr[...] = jnp.full(m_scr.shape, -jnp.inf, F32)
        l_scr[...] = jnp.zeros(l_scr.shape, F32)
        accf_scr[...] = jnp.zeros(accf_scr.shape, F32)
        accs_scr[...] = jnp.zeros(accs_scr.shape, F32)
        cf_scr[...] = jnp.zeros(cf_scr.shape, F32)
        cs_scr[...] = jnp.zeros(cs_scr.shape, F32)
        pn = _dot3_right(lfn_ref[0], u_incl)
        for i in range(n_new):
            cq_scr[SUBLANE * i:SUBLANE * (i + 1), :] = jnp.broadcast_to(pn[:, i:i + 1], (SUBLANE, page))
        attend([tuple(kvn_ref[0, sl] for sl in range(4))], [-pn], key <= qrow, key < qrow)

    blocks, biases = [], []
    cf = cf_scr[...]
    for i in range(pp):
        blocks.append(tuple(pages[i][0, sl].reshape(D_ATT, page).astype(BF16) for sl in range(4)))
        lf = lfts[i][0]
        suf = _dot3_right(lf, u_excl)
        biases.append(suf + cf)
        cf = cf + suf[:, 0:1] + lf[:, 0:1]
    cf_scr[...] = cf
    attend(blocks, biases, None, None)

    @pl.when(g == pl.num_programs(1) - 1)
    def _():
        hrow = lax.broadcasted_iota(jnp.int32, (SUBLANE, D_ATT), 0)
        hcol = lax.broadcasted_iota(jnp.int32, (SUBLANE, D_ATT), 1) // HEAD_DIM
        own = hrow == hcol
        inv_l = 1.0 / l_scr[...]
        for i in range(n_new):
            rs = slice(SUBLANE * i, SUBLANE * (i + 1))
            yf = accf_scr[rs, :] * inv_l[rs]
            yb_ref[0, i:i + 1, :] = jnp.sum(jnp.where(own, yf, 0.0), axis=0, keepdims=True)
            yc_ref[0, i:i + 1, :] = jnp.sum(jnp.where(own, accs_scr[rs, :], 0.0), axis=0, keepdims=True)


def _paged(page_table, cache_t, layer, cache_lft, qbd_f, qbd_s, kvn_t, lft_new, pp):
    db, n_pages = page_table.shape
    page = cache_t.shape[-1]
    n_new = qbd_f.shape[1] // SUBLANE
    nrow = SUBLANE * n_new
    ngrp = n_pages // pp

    def page_map(i, nd):
        return lambda b, g, pt: (layer, pt[b, n_pages - 1 - (g * pp + i)]) + (0,) * nd

    per_seq = lambda a: pl.BlockSpec((1,) + a.shape[1:], lambda b, g, pt: (b,) + (0,) * (a.ndim - 1))
    grid_spec = pltpu.PrefetchScalarGridSpec(
        num_scalar_prefetch=1,
        grid=(db, ngrp),
        in_specs=([pl.BlockSpec((1, 1, 4, H_ATT, HEAD_DIM, page), page_map(i, 4)) for i in range(pp)]
                  + [pl.BlockSpec((1, 1, SUBLANE, page), page_map(i, 2)) for i in range(pp)]
                  + [per_seq(qbd_f), per_seq(qbd_s), per_seq(kvn_t), per_seq(lft_new)]),
        out_specs=[pl.BlockSpec((1, n_new, D_ATT), lambda b, g, pt: (b, 0, 0))] * 2,
        scratch_shapes=[pltpu.VMEM((nrow, 1), F32), pltpu.VMEM((nrow, 1), F32),
                        pltpu.VMEM((nrow, D_ATT), F32), pltpu.VMEM((nrow, D_ATT), F32),
                        pltpu.VMEM((SUBLANE, 1), F32), pltpu.VMEM((nrow, 1), F32),
                        pltpu.VMEM((nrow, page), F32)])
    o = jax.ShapeDtypeStruct((db, n_new, D_ATT), F32)

    def body(pt_ref, *refs):
        pages = [r.at[0] for r in refs[:pp]]
        lfts = [r.at[0] for r in refs[pp:2 * pp]]
        _paged_body(pt_ref, *pages, *lfts, *refs[2 * pp:], pp=pp, page=page, n_new=n_new)

    return pl.pallas_call(
        body,
        grid_spec=grid_spec,
        out_shape=[o, o],
        compiler_params=_cparams(("arbitrary", "arbitrary")),
        name="paged_attn",
    )(page_table, *([cache_t] * pp), *([cache_lft] * pp), qbd_f, qbd_s, kvn_t, lft_new)


def _ffn_body(x_ref, ya_ref, yb_ref, yc_ref, woa_ref, wob_ref, woc_ref, g_ref, wg_ref, wu_ref, wd_ref,
              o_ref, h_scr, hn_scr, acc_scr):
    j = pl.program_id(1)

    @pl.when(j == 0)
    def _():
        h = (x_ref[...] + _dot(ya_ref[...].astype(BF16), woa_ref[...])
             + _dot(yb_ref[...].astype(BF16), wob_ref[...]) + _dot(yc_ref[...].astype(BF16), woc_ref[...]))
        h_scr[...] = h
        ms = jnp.mean(h * h, axis=-1, keepdims=True)
        hn_scr[...] = (h * lax.rsqrt(ms + RMS_EPS) * g_ref[...]).astype(BF16)
        acc_scr[...] = jnp.zeros(acc_scr.shape, F32)

    hn = hn_scr[...]
    gate = _dot(hn, wg_ref[...])
    up = _dot(hn, wu_ref[...])
    act = (gate * _sigmoid(gate) * up).astype(BF16)
    acc_scr[...] += _dot(act, wd_ref[...])

    @pl.when(j == pl.num_programs(1) - 1)
    def _():
        o_ref[...] = h_scr[...] + acc_scr[...]


def _ffn(x, ya, yb, yc, woa, wob, woc, g2, wg, wu, wd, tm, tf):
    n = x.shape[0]
    row = lambda a: pl.BlockSpec((tm, a.shape[1]), lambda i, j: (i, 0))
    full = lambda a: pl.BlockSpec(a.shape, lambda i, j: (0,) * a.ndim)
    return pl.pallas_call(
        _ffn_body,
        grid=(n // tm, D_FF // tf),
        in_specs=[row(x), row(ya), row(yb), row(yc), full(woa), full(wob), full(woc), full(g2),
                  pl.BlockSpec((D_MODEL, tf), lambda i, j: (0, j)),
                  pl.BlockSpec((D_MODEL, tf), lambda i, j: (0, j)),
                  pl.BlockSpec((tf, D_MODEL), lambda i, j: (j, 0))],
        out_specs=row(x),
        out_shape=jax.ShapeDtypeStruct((n, D_MODEL), F32),
        scratch_shapes=[pltpu.VMEM((tm, D_MODEL), F32), pltpu.VMEM((tm, D_MODEL), BF16),
                        pltpu.VMEM((tm, D_MODEL), F32)],
        compiler_params=_cparams(("arbitrary", "arbitrary")),
        name="merge_ffn",
    )(x, ya, yb, yc, woa, wob, woc, g2, wg, wu, wd)


def _head_slots(w):
    w = w.reshape(w.shape[:-1] + (H_ATT, 1, HEAD_DIM))
    half = (np.arange(H_ATT) % 2)[:, None] == np.arange(2)[None, :]
    w = jnp.where(jnp.asarray(half)[:, :, None], w, 0.0)
    return w.reshape(w.shape[:-3] + (QPAD,))


def _dense_heads(q):
    q = q.reshape(q.shape[:-1] + (H_ATT, 2, HEAD_DIM))
    half = (np.arange(H_ATT) % 2)[:, None] == np.arange(2)[None, :]
    return jnp.sum(jnp.where(jnp.asarray(half)[:, :, None], q, 0), axis=-2)


def _out_rows(w):
    w = w.reshape(H_ATT, 1, HEAD_DIM, w.shape[-1])
    half = ((np.arange(H_ATT) + 1) % 2)[:, None] == np.arange(2)[None, :]
    w = jnp.where(jnp.asarray(half)[:, :, None, None], w, 0.0)
    return w.reshape(QPAD, w.shape[-1])


def _block_diag(n, blk, val, limit=None):
    i = np.arange(n)
    m = (i[:, None] // blk == i[None, :] // blk).astype(np.float32) * val
    if limit is not None:
        m = m * ((i[:, None] < limit) & (i[None, :] < limit))
    return jnp.asarray(m, BF16)


def _pair_state(s):
    b = s.shape[0]
    s = s.reshape(b, N_PAIR, 2, HEAD_DIM, HEAD_DIM)
    return jnp.swapaxes(s, 2, 3).reshape(b, PAIR_ROWS, LANE)


def _unpair_state(s):
    b = s.shape[0]
    s = s.reshape(b, N_PAIR, HEAD_DIM, 2, HEAD_DIM)
    return jnp.swapaxes(s, 2, 3).reshape(b, H_RWKV, HEAD_DIM, HEAD_DIM)


def _pick(n, prefs):
    for p in prefs:
        if n % p == 0:
            return p
    return n


def kernel(x_prompt, x_sample, cache_kv, cache_logf, state_wkv, state_shift, page_table,
           norm1_g, w_in, shift_mu, rwkv_w0, rwkv_w2, rwkv_a0, rwkv_a2, rwkv_k_k, rwkv_k_a,
           rwkv_r_k, rwkv_ln_w, rwkv_ln_b, fox_b_f, fox_q_g, fox_k_g, w_out, norm2_g,
           ffn_w_gate, ffn_w_up, ffn_w_down):
    depth = w_in.shape[0]
    bp, tp, _ = x_prompt.shape
    db, ts, _ = x_sample.shape
    page = cache_kv.shape[2]

    o = N_SHIFT
    w_fq, w_fk, w_fv = (w_in[:, :, o + i * D_ATT:o + (i + 1) * D_ATT] for i in range(3))
    w_f = w_in[:, :, o + 3 * D_ATT:o + N_FOX_COLS]
    o = N_SHIFT + N_FOX_COLS
    w_sq, w_sk, w_sv = (w_in[:, :, o + i * D_ATT:o + (i + 1) * D_ATT] for i in range(3))
    w_proj = jnp.concatenate(
        [w_in[:, :, :N_SHIFT], w_fk, w_fv, w_sk, w_sv, _head_slots(w_fq), _head_slots(w_sq),
         jnp.pad(w_f, ((0, 0), (0, 0), (0, LANE - H_ATT)))], axis=-1).astype(BF16)
    bf_pad = jnp.pad(fox_b_f, ((0, 0), (0, LANE - H_ATT)))[:, None, :]
    kg_row = jnp.pad(jnp.tile(fox_k_g, (1, H_ATT)), ((0, 0), (0, 3 * LANE - D_ATT)))[:, None, :]
    qg_row = jnp.tile(jnp.tile(fox_q_g, (1, 2)), (1, H_ATT))[:, None, :]
    bdk = _block_diag(3 * LANE, HEAD_DIM, 1.0 / HEAD_DIM, limit=D_ATT)
    bdq = _block_diag(QPAD, LANE, 1.0 / HEAD_DIM)
    bd_sum = _block_diag(D_RWKV, HEAD_DIM, 1.0)
    bd_mean = _block_diag(D_RWKV, HEAD_DIM, 1.0 / HEAD_DIM)
    zl = jnp.zeros((depth, D_LORA, D_RWKV), F32)
    w2p = jnp.concatenate([rwkv_w2, zl], axis=1).astype(BF16)
    a2p = jnp.concatenate([zl, rwkv_a2], axis=1).astype(BF16)
    row3 = lambda a: a.reshape(depth, 1, -1)
    wo_a = w_out[:, :D_RWKV].astype(BF16)
    wo_b = w_out[:, D_RWKV:D_RWKV + D_ATT].astype(BF16)
    wo_c = w_out[:, D_RWKV + D_ATT:].astype(BF16)
    wo_b_slots, wo_c_slots = jax.vmap(_out_rows)(wo_b), jax.vmap(_out_rows)(wo_c)
    wg, wu, wd = ffn_w_gate.astype(BF16), ffn_w_up.astype(BF16), ffn_w_down.astype(BF16)

    tq = _pick(tp, (1024, 512, 256, 128))
    tk = _pick(tq, (512, 256, 128))
    tc_sum = _pick(tp, (256, 128))
    tri = jnp.asarray(np.tril(np.ones((tc_sum, tc_sum), np.float32)), BF16)
    sk = _pick(tk, (256, 128))
    u_sb = jnp.asarray(np.tile(np.tril(np.ones((sk, sk), np.float32), -1), (2, 1)), BF16)

    cache_t = jnp.transpose(cache_kv, (0, 1, 3, 4, 5, 2))
    cache_lft = jnp.pad(jnp.swapaxes(cache_logf, 2, 3), ((0, 0), (0, 0), (0, SUBLANE - H_ATT), (0, 0)))
    eye_h = jnp.asarray(np.eye(SUBLANE, H_ATT, dtype=np.float32), BF16)

    np_rows, ns_rows = bp * tp, db * ts
    tm_p = _pick(np_rows, (256, 128))
    tm_ffn = _pick(np_rows, (1024, 512, 256, 128))
    tm_s = _pick(ns_rows, (256, 128, 64, 32, 16, 8))
    tc_scan = _pick(tp, (256, 128, 64, 32, 16, 8))
    nb_s = _pick(db, (4, 2, 1))
    pp = _pick(page_table.shape[1], (8, 4, 2, 1))
    tf = _pick(D_FF, (256, 128))
    ts_pad = -(-ts // SUBLANE) * SUBLANE

    xp = x_prompt.reshape(np_rows, D_MODEL)
    xs = x_sample.reshape(ns_rows, D_MODEL)
    outs = [[] for _ in range(8)]

    def rwkv(feats, prev, s0p, l, nb, tc, tm):
        nseq, t, _ = feats.shape
        rows = nseq * t
        shifted = jnp.concatenate([prev[:, None, :], feats[:, :-1]], axis=1)
        res = _prep(feats.reshape(rows, N_SHIFT), shifted.reshape(rows, N_SHIFT),
                    row3(shift_mu)[l], row3(rwkv_w0)[l], w2p[l], row3(rwkv_a0)[l], a2p[l],
                    row3(rwkv_k_k)[l], row3(rwkv_k_a)[l], row3(rwkv_r_k)[l], bd_sum, tm)
        seq = [a.reshape(nseq, t, D_RWKV) for a in res[:6]]
        t_pad = -(-t // SUBLANE) * SUBLANE
        if t_pad != t:
            pad = lambda a, v: jnp.pad(a, ((0, 0), (0, t_pad - t), (0, 0)), constant_values=v)
            seq = [pad(a, 1.0 if i == 1 else 0.0) for i, a in enumerate(seq)]
        y, s_fin = _scan(*seq, s0p, nb, min(tc, t_pad))
        y = y[:, :t].reshape(rows, D_RWKV)
        ya = _post(y, res[6], res[7], row3(rwkv_ln_w)[l], row3(rwkv_ln_b)[l], bd_mean, tm)
        return ya, s_fin

    for l in range(depth):
        proj_consts = (row3(norm1_g)[l], w_proj[l], bdk, bdq, kg_row[l], qg_row[l], bf_pad[l])
        ffn_tail = (row3(norm2_g)[l], wg[l], wu[l], wd[l])

        feats, kv, kvb, qf, qs, lf = _proj(xp, *proj_consts, tm_p)
        feats3 = feats.reshape(bp, tp, N_SHIFT)
        ya, s_fin = rwkv(feats3, jnp.zeros((bp, N_SHIFT), F32), jnp.zeros((bp, PAIR_ROWS, LANE), F32),
                         l, bp, tc_scan, tm_p)
        cum, cumt = _cumsum(lf.reshape(bp, tp, LANE), tri)
        kvb3 = kvb.reshape(bp, tp, KV_W)
        yb = _fox(qf.reshape(bp, tp, QPAD), kvb3, cum, cumt, tq, tk)
        yc = _sb(qs.reshape(bp, tp, QPAD), kvb3, u_sb, tq, tk)
        xp = _ffn(xp, ya, yb.reshape(np_rows, QPAD), yc.reshape(np_rows, QPAD),
                  wo_a[l], wo_b_slots[l], wo_c_slots[l], *ffn_tail, tm_ffn, tf)
        outs[0].append(kv.reshape(bp, tp, 4, H_ATT, HEAD_DIM))
        outs[1].append(lf[:, :H_ATT].reshape(bp, tp, H_ATT))
        outs[2].append(_unpair_state(s_fin))
        outs[3].append(feats3[:, -1])

        feats, kv, kvb, qf, qs, lf = _proj(xs, *proj_consts, tm_s)
        feats3 = feats.reshape(db, ts, N_SHIFT)
        ya, s_fin = rwkv(feats3, state_shift[l], _pair_state(state_wkv[l]), l, nb_s, ts_pad, tm_s)

        def score_rows(q):
            q = _dense_heads(q.reshape(db, ts, QPAD))
            q = q[:, :, None, :, :] * eye_h[None, None, :, :, None]
            return q.reshape(db, ts * SUBLANE, D_ATT)

        kvn_t = jnp.pad(jnp.transpose(kvb.reshape(db, ts, 4, D_ATT), (0, 2, 3, 1)),
                        ((0, 0), (0, 0), (0, 0), (0, page - ts)))
        lft_new = jnp.pad(jnp.swapaxes(lf.reshape(db, ts, LANE)[:, :, :SUBLANE], 1, 2),
                          ((0, 0), (0, 0), (0, page - ts)))
        yb, yc = _paged(page_table, cache_t, l, cache_lft, score_rows(qf), score_rows(qs), kvn_t, lft_new, pp)
        xs = _ffn(xs, ya, yb.reshape(ns_rows, D_ATT), yc.reshape(ns_rows, D_ATT),
                  wo_a[l], wo_b[l], wo_c[l], *ffn_tail, tm_s, tf)
        outs[4].append(kv.reshape(db, ts, 4, H_ATT, HEAD_DIM))
        outs[5].append(lf[:, :H_ATT].reshape(db, ts, H_ATT))
        outs[6].append(_unpair_state(s_fin))
        outs[7].append(feats3[:, -1])

    return (xp.reshape(bp, tp, D_MODEL), xs.reshape(db, ts, D_MODEL)) + tuple(jnp.stack(o) for o in outs)
```

```python
import functools

import numpy as np
import jax
import jax.numpy as jnp
from jax import lax
from jax.experimental import pallas as pl
from jax.experimental.pallas import tpu as pltpu

F32 = jnp.float32
BF16 = jnp.bfloat16

D_MODEL = 1024
HEAD_DIM = 64
H_RWKV = 6
H_ATT = 5
D_RWKV = H_RWKV * HEAD_DIM
D_ATT = H_ATT * HEAD_DIM
D_LORA = 64
N_SHIFT = 4 * D_RWKV + 2 * D_LORA
N_FOX_COLS = 3 * D_ATT + H_ATT
D_FF = 2816
KV_W = 4 * D_ATT
RMS_EPS = 1e-6
GN_EPS = 64e-5
ATTN_SCALE = HEAD_DIM ** -0.5

LANE = 128
SUBLANE = 8
QPAD = H_ATT * LANE
N_PAIR = H_RWKV // 2
PAIR_ROWS = N_PAIR * HEAD_DIM

C_FEATS = 0
C_KV = N_SHIFT
C_QF = C_KV + KV_W
C_QS = C_QF + QPAD
C_F = C_QS + QPAD
N_PROJ = C_F + LANE

VMEM_LIMIT = 56 * 1024 * 1024


def _cparams(sem):
    return pltpu.CompilerParams(dimension_semantics=sem, vmem_limit_bytes=VMEM_LIMIT)


def _dot(a, b):
    return jnp.dot(a, b, preferred_element_type=F32)


def _nt_dot(a, b):
    return lax.dot_general(a, b, (((1,), (1,)), ((), ())), preferred_element_type=F32)


def _split3(x):
    hi = x.astype(BF16)
    r1 = x - hi.astype(F32)
    mid = r1.astype(BF16)
    lo = (r1 - mid.astype(F32)).astype(BF16)
    return hi, mid, lo


def _dot3_right(x, m):
    hi, mid, lo = _split3(x)
    return _dot(hi, m) + _dot(mid, m) + _dot(lo, m)


def _dot3_left(m, x):
    hi, mid, lo = _split3(x)
    return _dot(m, hi) + _dot(m, mid) + _dot(m, lo)


def _split2(x):
    hi = x.astype(BF16)
    return hi, (x - hi.astype(F32)).astype(BF16)


def _dot2_right(x, m):
    hi, lo = _split2(x)
    return _dot(hi, m) + _dot(lo, m)


def _softplus(x):
    return jnp.maximum(x, 0.0) + jnp.log1p(jnp.exp(-jnp.abs(x)))


def _sigmoid(x):
    return 1.0 / (1.0 + jnp.exp(-x))


def _proj_body(x_ref, g_ref, w_ref, bdk_ref, bdq_ref, kg_ref, qg_ref, bf_ref,
               feats_ref, kv_ref, kvb_ref, qf_ref, qs_ref, lf_ref):
    x = x_ref[...]
    ms = jnp.mean(x * x, axis=-1, keepdims=True)
    xn = (x * lax.rsqrt(ms + RMS_EPS) * g_ref[...]).astype(BF16)

    feats_ref[...] = _dot(xn, w_ref[:, C_FEATS:C_KV])

    ukv = _dot(xn, w_ref[:, C_KV:C_QF])
    ufk = ukv[:, :3 * LANE]
    msk = _dot3_right(ufk * ufk, bdk_ref[...])
    col = lax.broadcasted_iota(jnp.int32, ufk.shape, 1)
    kv0 = jnp.where(col < D_ATT, ufk * lax.rsqrt(msk + RMS_EPS) * kg_ref[...], ufk)
    kv_ref[:, :3 * LANE] = kv0
    kv_ref[:, 3 * LANE:] = ukv[:, 3 * LANE:]
    kvb_ref[:, :3 * LANE] = kv0.astype(BF16)
    kvb_ref[:, 3 * LANE:] = ukv[:, 3 * LANE:].astype(BF16)

    uqf = _dot(xn, w_ref[:, C_QF:C_QS])
    msq = _dot3_right(uqf * uqf, bdq_ref[...])
    qf_ref[...] = (uqf * lax.rsqrt(msq + RMS_EPS) * qg_ref[...] * ATTN_SCALE).astype(BF16)

    uqs = _dot(xn, w_ref[:, C_QS:C_F])
    qs_ref[...] = (uqs * ATTN_SCALE).astype(BF16)

    uf = _dot(xn, w_ref[:, C_F:N_PROJ]) + bf_ref[...]
    lf_ref[...] = -_softplus(-uf)


def _proj(x, g, w, bdk, bdq, kg, qg, bf, tm):
    n = x.shape[0]
    full = lambda a: pl.BlockSpec(a.shape, lambda i: (0,) * a.ndim)
    row = lambda c: pl.BlockSpec((tm, c), lambda i: (i, 0))
    return pl.pallas_call(
        _proj_body,
        grid=(n // tm,),
        in_specs=[row(D_MODEL), full(g), full(w), full(bdk), full(bdq), full(kg), full(qg), full(bf)],
        out_specs=[row(N_SHIFT), row(KV_W), row(KV_W), row(QPAD), row(QPAD), row(LANE)],
        out_shape=[jax.ShapeDtypeStruct((n, N_SHIFT), F32),
                   jax.ShapeDtypeStruct((n, KV_W), F32),
                   jax.ShapeDtypeStruct((n, KV_W), BF16),
                   jax.ShapeDtypeStruct((n, QPAD), BF16),
                   jax.ShapeDtypeStruct((n, QPAD), BF16),
                   jax.ShapeDtypeStruct((n, LANE), F32)],
        compiler_params=_cparams(("arbitrary",)),
        name="proj",
    )(x, g, w, bdk, bdq, kg, qg, bf)


def _prep_body(f_ref, sh_ref, prev_ref, mu_ref, w0_ref, w2_ref, a0_ref, a2_ref, kk_ref, ka_ref, rk_ref, bd_ref,
               nkk_ref, w_ref, kka_ref, k_ref, r_ref, v_ref, bonus_ref, gate_ref, *, tm, seq_len):
    f = f_ref[...]
    if seq_len is None:
        shifted = sh_ref[...]
    else:
        starts_sequence = (pl.program_id(0) * tm) % seq_len == 0
        first = jnp.where(starts_sequence, prev_ref[0], sh_ref[SUBLANE - 1:SUBLANE, :])
        row = lax.broadcasted_iota(jnp.int32, f.shape, 0)
        shifted = jnp.where(row == 0, first, pltpu.roll(f, 1, 0))
    m = f + (shifted - f) * mu_ref[...]
    r = m[:, 0:D_RWKV]
    k = m[:, D_RWKV:2 * D_RWKV]
    v = m[:, 2 * D_RWKV:3 * D_RWKV]
    g = m[:, 3 * D_RWKV:4 * D_RWKV]
    lo = m[:, 4 * D_RWKV:N_SHIFT]
    lw = _dot(jnp.tanh(lo).astype(BF16), w2_ref[...])
    la = _dot(lo.astype(BF16), a2_ref[...])
    w = -_softplus(-(w0_ref[...] + lw)) - 0.5
    decay = jnp.exp(-jnp.exp(w))
    a = _sigmoid(a0_ref[...] + la)
    kk = k * kk_ref[...]
    ss = _dot3_right(kk * kk, bd_ref[...])
    kk = kk / jnp.maximum(jnp.sqrt(ss), 1e-12)
    k2 = k * (1.0 + (a - 1.0) * ka_ref[...])
    rks = _dot3_right(r * k2 * rk_ref[...], bd_ref[...])
    nkk_ref[...] = -kk
    w_ref[...] = decay
    kka_ref[...] = kk * a
    k_ref[...] = k2
    r_ref[...] = r
    v_ref[...] = v
    bonus_ref[...] = rks * v
    gate_ref[...] = _sigmoid(g)


def _prep(feats, shifted, prev, mu, w0, w2p, a0, a2p, k_k, k_a, r_k, bd, tm, seq_len):
    n = feats.shape[0]
    full = lambda a: pl.BlockSpec(a.shape, lambda i: (0,) * a.ndim)
    blk = lambda c: pl.BlockSpec((tm, c), lambda i: (i, 0))
    if seq_len is None:
        sh_spec = blk(N_SHIFT)
        prev_spec = pl.BlockSpec((1, 1, N_SHIFT), lambda i: (0, 0, 0))
    else:
        sh_spec = pl.BlockSpec((SUBLANE, N_SHIFT), lambda i: (jnp.maximum(i * (tm // SUBLANE) - 1, 0), 0))
        prev_spec = pl.BlockSpec((1, 1, N_SHIFT), lambda i: ((i * tm) // seq_len, 0, 0))
    o384 = jax.ShapeDtypeStruct((n, D_RWKV), F32)
    return pl.pallas_call(
        functools.partial(_prep_body, tm=tm, seq_len=seq_len),
        grid=(n // tm,),
        in_specs=[blk(N_SHIFT), sh_spec, prev_spec]
        + [full(a) for a in (mu, w0, w2p, a0, a2p, k_k, k_a, r_k, bd)],
        out_specs=[blk(D_RWKV)] * 8,
        out_shape=[o384] * 8,
        compiler_params=_cparams(("arbitrary",)),
        name="rwkv_prep",
    )(feats, shifted, prev, mu, w0, w2p, a0, a2p, k_k, k_a, r_k, bd)


def _scan_body(nkk_ref, w_ref, kka_ref, k_ref, r_ref, v_ref, s0_ref, ones_ref, bv_ref,
               y_ref, sT_ref, s_scr, *, nb, tc):
    c = pl.program_id(1)

    @pl.when(c == 0)
    def _():
        s_scr[...] = s0_ref[...]

    ones2 = ones_ref[...]
    zpad = jnp.zeros((LANE - 3 * SUBLANE, LANE), F32)
    ztile = jnp.zeros((HEAD_DIM, LANE), F32)
    lane_id = lax.broadcasted_iota(jnp.int32, (PAIR_ROWS, LANE), 1)
    lane_in_half = lane_id % HEAD_DIM
    left = lane_id < HEAD_DIM

    def hilo(z):
        return jnp.concatenate(_split2(z), axis=1)

    def bc(x, u):
        return jnp.concatenate(
            [jnp.broadcast_to(x[u:u + 1, j * LANE:(j + 1) * LANE], (HEAD_DIM, LANE))
             for j in range(N_PAIR)], axis=0)

    def group(i, carry):
        t0 = pl.multiple_of(i * SUBLANE, SUBLANE)
        lhs = []
        for b in range(nb):
            vr = v_ref[b, pl.ds(t0, SUBLANE), :]
            for j in range(N_PAIR):
                hi, mid, lo = _split3(vr[:, j * LANE:(j + 1) * LANE])
                stacked = jnp.concatenate([hi.astype(F32), mid.astype(F32), lo.astype(F32), zpad], axis=0)
                at = stacked.T.astype(BF16)
                lhs.append(jnp.concatenate([at[:HEAD_DIM], at[HEAD_DIM:]], axis=1))
        vt_all = _dot(jnp.concatenate(lhs, axis=0), bv_ref[...])

        rows = [[ref[b, pl.ds(t0, SUBLANE), :] for ref in (nkk_ref, w_ref, kka_ref, k_ref, r_ref)]
                for b in range(nb)]
        s = [s_scr[b] for b in range(nb)]
        g_acc = [jnp.zeros((PAIR_ROWS, LANE), F32) for _ in range(nb)]
        for u in range(SUBLANE):
            for b in range(nb):
                nkk, w, kka, k, r = rows[b]
                z = s[b] * bc(nkk, u)
                sa = jnp.where(left,
                               jnp.sum(jnp.where(left, z, 0.0), axis=1, keepdims=True),
                               jnp.sum(jnp.where(left, 0.0, z), axis=1, keepdims=True))
                vt = vt_all[b * PAIR_ROWS:(b + 1) * PAIR_ROWS, u * LANE:(u + 1) * LANE]
                s[b] = s[b] * bc(w, u) + sa * bc(kka, u) + vt * bc(k, u)
                yt = _dot(hilo(s[b] * bc(r, u)), ones2)
                g_acc[b] = jnp.where(lane_in_half == u, yt, g_acc[b])
        for b in range(nb):
            s_scr[b] = s[b]
            for j in range(N_PAIR):
                gj = g_acc[b][j * HEAD_DIM:(j + 1) * HEAD_DIM]
                tt = jnp.concatenate([gj, ztile], axis=0).T
                y_ref[b, pl.ds(t0, SUBLANE), j * LANE:(j + 1) * LANE] = (
                    tt[:SUBLANE] + pltpu.roll(tt[HEAD_DIM:HEAD_DIM + SUBLANE], HEAD_DIM, 1))
        return carry

    lax.fori_loop(0, tc // SUBLANE, group, 0)

    @pl.when(c == pl.num_programs(1) - 1)
    def _():
        sT_ref[...] = s_scr[...]


def _scan_consts():
    k = np.arange(2 * LANE)
    seg_k = (k % LANE) // HEAD_DIM
    lane = np.arange(LANE)
    ones2 = (seg_k[:, None] == lane[None, :] // HEAD_DIM)
    cc = k % LANE
    col = np.arange(SUBLANE * LANE)
    bv = ((cc[:, None] < 3 * SUBLANE) & (cc[:, None] % SUBLANE == col[None, :] // LANE)
          & ((k[:, None] // LANE) == (col[None, :] % LANE) // HEAD_DIM))
    as_bf = lambda a: jnp.asarray(a.astype(np.float32), BF16)
    return as_bf(ones2), as_bf(bv)


def _scan(nkk, w, kka, k, r, v, s0, nb, tc):
    b, t, _ = r.shape
    ones2, bv = _scan_consts()
    blk = pl.BlockSpec((nb, tc, D_RWKV), lambda i, c: (i, c, 0))
    sblk = pl.BlockSpec((nb, PAIR_ROWS, LANE), lambda i, c: (i, 0, 0))
    full = lambda a: pl.BlockSpec(a.shape, lambda i, c: (0,) * a.ndim)
    return pl.pallas_call(
        functools.partial(_scan_body, nb=nb, tc=tc),
        grid=(b // nb, t // tc),
        in_specs=[blk] * 6 + [sblk, full(ones2), full(bv)],
        out_specs=[blk, sblk],
        out_shape=[jax.ShapeDtypeStruct((b, t, D_RWKV), F32),
                   jax.ShapeDtypeStruct((b, PAIR_ROWS, LANE), F32)],
        scratch_shapes=[pltpu.VMEM((nb, PAIR_ROWS, LANE), F32)],
        compiler_params=_cparams(("arbitrary", "arbitrary")),
        name="rwkv_scan",
    )(nkk, w, kka, k, r, v, s0, ones2, bv)


def _post_body(y_ref, bonus_ref, gate_ref, lnw_ref, lnb_ref, bd_ref, o_ref):
    y = y_ref[...]
    mean = _dot3_right(y, bd_ref[...])
    d = y - mean
    var = _dot3_right(d * d, bd_ref[...])
    yn = d * lax.rsqrt(var + GN_EPS) * lnw_ref[...] + lnb_ref[...]
    o_ref[...] = ((yn + bonus_ref[...]) * gate_ref[...]).astype(BF16)


def _post(y, bonus, gate, lnw, lnb, bdm, tm):
    n = y.shape[0]
    full = lambda a: pl.BlockSpec(a.shape, lambda i: (0,) * a.ndim)
    blk = pl.BlockSpec((tm, D_RWKV), lambda i: (i, 0))
    return pl.pallas_call(
        _post_body,
        grid=(n // tm,),
        in_specs=[blk, blk, blk, full(lnw), full(lnb), full(bdm)],
        out_specs=blk,
        out_shape=jax.ShapeDtypeStruct((n, D_RWKV), BF16),
        compiler_params=_cparams(("arbitrary",)),
        name="rwkv_post",
    )(y, bonus, gate, lnw, lnb, bdm)


def _cumsum_body(x_ref, tri_ref, cum_ref, cumt_ref, carry):
    @pl.when(pl.program_id(1) == 0)
    def _():
        carry[...] = jnp.zeros(carry.shape, F32)

    c = _dot3_left(tri_ref[...], x_ref[0]) + carry[...]
    cum_ref[0] = c
    cumt_ref[0] = c.T[0:SUBLANE, :]
    n = c.shape[0]
    carry[...] = c[n - 1:n, :]


def _cumsum(lf, tri):
    b, t, _ = lf.shape
    tc = tri.shape[0]
    return pl.pallas_call(
        _cumsum_body,
        grid=(b, t // tc),
        in_specs=[pl.BlockSpec((1, tc, LANE), lambda i, j: (i, j, 0)),
                  pl.BlockSpec(tri.shape, lambda i, j: (0, 0))],
        out_specs=[pl.BlockSpec((1, tc, LANE), lambda i, j: (i, j, 0)),
                   pl.BlockSpec((1, SUBLANE, tc), lambda i, j: (i, 0, j))],
        out_shape=[jax.ShapeDtypeStruct((b, t, LANE), F32), jax.ShapeDtypeStruct((b, SUBLANE, t), F32)],
        scratch_shapes=[pltpu.VMEM((1, LANE), F32)],
        compiler_params=_cparams(("arbitrary", "arbitrary")),
        name="logf_cumsum",
    )(lf, tri)


def _fox_body(q_ref, k_ref, v_ref, cum_ref, cumt_ref, o_ref, *, tq, tk):
    h = pl.program_id(1)
    qi = pl.program_id(2)
    nd = tq // tk
    lane = lax.broadcasted_iota(jnp.int32, (tq, LANE), 1)
    cq = jnp.sum(jnp.where(lane == h, cum_ref[0], 0.0), axis=1, keepdims=True)
    sub = lax.broadcasted_iota(jnp.int32, (SUBLANE, tk), 0)
    q = q_ref[0]
    rowp = qi * tq + lax.broadcasted_iota(jnp.int32, (tq, tk), 0)
    colp = lax.broadcasted_iota(jnp.int32, (tq, tk), 1)

    def block(j, carry, masked):
        k0 = pl.multiple_of(j * tk, tk)
        kw = k_ref[0, pl.ds(k0, tk), :]
        vw = v_ref[0, pl.ds(k0, tk), :]
        ck = jnp.sum(jnp.where(sub == h, cumt_ref[0, :, pl.ds(k0, tk)], 0.0), axis=0, keepdims=True)
        m_old, l_old, acc = carry
        s = _nt_dot(q, kw) + cq - ck
        if masked:
            s = jnp.where(colp + k0 <= rowp, s, -jnp.inf)
        m_new = jnp.maximum(m_old, jnp.max(s, axis=1, keepdims=True))
        alpha = jnp.exp(m_old - m_new)
        pr = jnp.exp(s - m_new)
        l_new = alpha * l_old + jnp.sum(pr, axis=1, keepdims=True)
        return m_new, l_new, alpha * acc + _dot(pr.astype(BF16), vw)

    carry = (jnp.full((tq, 1), -jnp.inf, F32), jnp.zeros((tq, 1), F32), jnp.zeros((tq, LANE), F32))
    carry = lax.fori_loop(0, qi * nd, lambda j, c: block(j, c, False), carry)
    for jj in range(nd):
        carry = block(qi * nd + jj, carry, True)
    lane_o = lax.broadcasted_iota(jnp.int32, (tq, LANE), 1)
    valid = (lane_o >= HEAD_DIM) == ((h + 1) % 2 == 1)
    o_ref[0] = jnp.where(valid, carry[2] / carry[1], 0.0).astype(BF16)


def _sb_body(q_ref, k_ref, v_ref, u_ref, o_ref, *, tq, tk):
    h = pl.program_id(1)
    qi = pl.program_id(2)
    nd = tq // tk
    sk = u_ref.shape[1]
    q = q_ref[0]
    rowp = qi * tq + lax.broadcasted_iota(jnp.int32, (tq, tk), 0)
    colp = lax.broadcasted_iota(jnp.int32, (tq, tk), 1)

    def block(j, carry, masked):
        k0 = pl.multiple_of(j * tk, tk)
        kw = k_ref[0, pl.ds(k0, tk), :]
        vw = v_ref[0, pl.ds(k0, tk), :]
        c_run, acc = carry
        z = _nt_dot(q, kw)
        x = jnp.minimum(-z, 0.0) - jnp.log(1.0 + jnp.exp(-jnp.abs(z)))
        zx = x + z
        if masked:
            mask = colp + k0 < rowp
            x = jnp.where(mask, x, 0.0)
        ws = []
        for sb in reversed(range(tk // sk)):
            xs = x[:, sb * sk:(sb + 1) * sk]
            suf = _dot(jnp.concatenate(_split2(xs), axis=1), u_ref[...]) + c_run
            ws.append(jnp.exp(zx[:, sb * sk:(sb + 1) * sk] + suf))
            c_run = suf[:, 0:1] + xs[:, 0:1]
        w = jnp.concatenate(ws[::-1], axis=1)
        if masked:
            w = jnp.where(mask, w, 0.0)
        return c_run, acc + _dot(w.astype(BF16), vw)

    carry = (jnp.zeros((tq, 1), F32), jnp.zeros((tq, LANE), F32))
    for jj in reversed(range(nd)):
        carry = block(qi * nd + jj, carry, True)
    carry = lax.fori_loop(0, qi * nd, lambda i, c: block(qi * nd - 1 - i, c, False), carry)
    lane_o = lax.broadcasted_iota(jnp.int32, (tq, LANE), 1)
    valid = (lane_o >= HEAD_DIM) == ((h + 1) % 2 == 1)
    o_ref[0] = jnp.where(valid, carry[1], 0.0).astype(BF16)


def _attn_specs(tq, t, group):
    g0 = group * (2 * D_ATT // LANE)
    return ([pl.BlockSpec((1, tq, LANE), lambda b, h, i: (b, i, h)),
             pl.BlockSpec((1, t, LANE), lambda b, h, i: (b, 0, g0 + h // 2)),
             pl.BlockSpec((1, t, LANE), lambda b, h, i: (b, 0, g0 + (H_ATT + h) // 2))],
            pl.BlockSpec((1, tq, LANE), lambda b, h, i: (b, i, h)))


def _fox(q, kvb, cum, cumt, tq, tk):
    b, t, _ = q.shape
    ins, out = _attn_specs(tq, t, 0)
    return pl.pallas_call(
        functools.partial(_fox_body, tq=tq, tk=tk),
        grid=(b, H_ATT, t // tq),
        in_specs=ins + [pl.BlockSpec((1, tq, LANE), lambda b, h, i: (b, i, 0)),
                        pl.BlockSpec((1, SUBLANE, t), lambda b, h, i: (b, 0, 0))],
        out_specs=out,
        out_shape=jax.ShapeDtypeStruct((b, t, QPAD), BF16),
        compiler_params=_cparams(("arbitrary", "arbitrary", "arbitrary")),
        name="fox_attn",
    )(q, kvb, kvb, cum, cumt)


def _sb(q, kvb, u, tq, tk):
    b, t, _ = q.shape
    ins, out = _attn_specs(tq, t, 1)
    return pl.pallas_call(
        functools.partial(_sb_body, tq=tq, tk=tk),
        grid=(b, H_ATT, t // tq),
        in_specs=ins + [pl.BlockSpec(u.shape, lambda b, h, i: (0, 0))],
        out_specs=out,
        out_shape=jax.ShapeDtypeStruct((b, t, QPAD), BF16),
        compiler_params=_cparams(("arbitrary", "arbitrary", "arbitrary")),
        name="sb_attn",
    )(q, kvb, kvb, u)


def _paged_body(pt_ref, *refs, pp, page, n_new):
    pages = refs[:pp]
    lfts = refs[pp:2 * pp]
    qf_ref, qs_ref, kvn_ref, lfn_ref, yb_ref, yc_ref = refs[2 * pp:2 * pp + 6]
    m_scr, l_scr, accf_scr, accs_scr, cf_scr, cs_scr, cq_scr = refs[2 * pp + 6:]
    g = pl.program_id(1)
    nrow = SUBLANE * n_new

    ji = lax.broadcasted_iota(jnp.int32, (page, page), 0)
    si = lax.broadcasted_iota(jnp.int32, (page, page), 1)
    u_excl = (ji > si).astype(BF16)
    u_incl = (ji <= si).astype(BF16)
    key = lax.broadcasted_iota(jnp.int32, (nrow, page), 1)
    qrow = lax.broadcasted_iota(jnp.int32, (nrow, page), 0) // SUBLANE

    def attend(blocks, biases, mask_f, mask_s):
        n = len(blocks)
        sf = [_dot(qf_ref[0], blk[0]) + cq_scr[...] + jnp.concatenate([bias] * n_new, axis=0)
              for blk, bias in zip(blocks, biases)]
        if mask_f is not None:
            sf = [jnp.where(mask_f, s, -jnp.inf) for s in sf]
        m_old = m_scr[...]
        m_new = jnp.maximum(m_old, jnp.max(functools.reduce(jnp.maximum, sf), axis=1, keepdims=True))
        alpha = jnp.exp(m_old - m_new)
        prs = [jnp.exp(s - m_new) for s in sf]
        l_scr[...] = alpha * l_scr[...] + jnp.sum(functools.reduce(jnp.add, prs), axis=1, keepdims=True)
        pv = functools.reduce(jnp.add, [_nt_dot(p.astype(BF16), blk[1]) for p, blk in zip(prs, blocks)])
        accf_scr[...] = alpha * accf_scr[...] + pv
        m_scr[...] = m_new

        zs = [_dot(qs_ref[0], blk[2]) for blk in blocks]
        xs = [-_softplus(z) for z in zs]
        if mask_s is not None:
            xs = [jnp.where(mask_s, x, 0.0) for x in xs]
        sufs = [_dot3_right(x, u_excl) for x in xs]
        c = cs_scr[...]
        av = None
        for z, x, suf, blk in zip(zs, xs, sufs, blocks):
            a = jnp.exp(z + x + suf + c)
            if mask_s is not None:
                a = jnp.where(mask_s, a, 0.0)
            d = _nt_dot(a.astype(BF16), blk[3])
            av = d if av is None else av + d
            c = c + suf[:, 0:1] + x[:, 0:1]
        accs_scr[...] = accs_scr[...] + av
        cs_scr[...] = c

    @pl.when(g == 0)
    def _():
        m_scr[...] = jnp.full(m_scr.shape, -jnp.inf, F32)
        l_scr[...] = jnp.zeros(l_scr.shape, F32)
        accf_scr[...] = jnp.zeros(accf_scr.shape, F32)
        accs_scr[...] = jnp.zeros(accs_scr.shape, F32)
        cf_scr[...] = jnp.zeros(cf_scr.shape, F32)
        cs_scr[...] = jnp.zeros(cs_scr.shape, F32)
        pn = _dot3_right(lfn_ref[0], u_incl)
        for i in range(n_new):
            cq_scr[SUBLANE * i:SUBLANE * (i + 1), :] = jnp.broadcast_to(pn[:, i:i + 1], (SUBLANE, page))
        attend([tuple(kvn_ref[0, sl] for sl in range(4))], [-pn], key <= qrow, key < qrow)

    blocks, biases = [], []
    cf = cf_scr[...]
    for i in range(pp):
        blocks.append(tuple(pages[i][0, sl].reshape(D_ATT, page).astype(BF16) for sl in range(4)))
        lf = lfts[i][0]
        suf = _dot3_right(lf, u_excl)
        biases.append(suf + cf)
        cf = cf + suf[:, 0:1] + lf[:, 0:1]
    cf_scr[...] = cf
    attend(blocks, biases, None, None)

    @pl.when(g == pl.num_programs(1) - 1)
    def _():
        hrow = lax.broadcasted_iota(jnp.int32, (SUBLANE, D_ATT), 0)
        hcol = lax.broadcasted_iota(jnp.int32, (SUBLANE, D_ATT), 1) // HEAD_DIM
        own = hrow == hcol
        inv_l = 1.0 / l_scr[...]
        for i in range(n_new):
            rs = slice(SUBLANE * i, SUBLANE * (i + 1))
            yf = accf_scr[rs, :] * inv_l[rs]
            yb_ref[0, i:i + 1, :] = jnp.sum(jnp.where(own, yf, 0.0), axis=0, keepdims=True)
            yc_ref[0, i:i + 1, :] = jnp.sum(jnp.where(own, accs_scr[rs, :], 0.0), axis=0, keepdims=True)


def _paged(page_table, cache_t, layer, cache_lft, qbd_f, qbd_s, kvn_t, lft_new, pp):
    db, n_pages = page_table.shape
    page = cache_t.shape[-1]
    n_new = qbd_f.shape[1] // SUBLANE
    nrow = SUBLANE * n_new
    ngrp = n_pages // pp

    def page_map(i, nd):
        return lambda b, g, pt: (layer, pt[b, n_pages - 1 - (g * pp + i)]) + (0,) * nd

    per_seq = lambda a: pl.BlockSpec((1,) + a.shape[1:], lambda b, g, pt: (b,) + (0,) * (a.ndim - 1))
    grid_spec = pltpu.PrefetchScalarGridSpec(
        num_scalar_prefetch=1,
        grid=(db, ngrp),
        in_specs=([pl.BlockSpec((1, 1, 4, H_ATT, HEAD_DIM, page), page_map(i, 4)) for i in range(pp)]
                  + [pl.BlockSpec((1, 1, SUBLANE, page), page_map(i, 2)) for i in range(pp)]
                  + [per_seq(qbd_f), per_seq(qbd_s), per_seq(kvn_t), per_seq(lft_new)]),
        out_specs=[pl.BlockSpec((1, n_new, D_ATT), lambda b, g, pt: (b, 0, 0))] * 2,
        scratch_shapes=[pltpu.VMEM((nrow, 1), F32), pltpu.VMEM((nrow, 1), F32),
                        pltpu.VMEM((nrow, D_ATT), F32), pltpu.VMEM((nrow, D_ATT), F32),
                        pltpu.VMEM((SUBLANE, 1), F32), pltpu.VMEM((nrow, 1), F32),
                        pltpu.VMEM((nrow, page), F32)])
    o = jax.ShapeDtypeStruct((db, n_new, D_ATT), F32)

    def body(pt_ref, *refs):
        pages = [r.at[0] for r in refs[:pp]]
        lfts = [r.at[0] for r in refs[pp:2 * pp]]
        _paged_body(pt_ref, *pages, *lfts, *refs[2 * pp:], pp=pp, page=page, n_new=n_new)

    return pl.pallas_call(
        body,
        grid_spec=grid_spec,
        out_shape=[o, o],
        compiler_params=_cparams(("arbitrary", "arbitrary")),
        name="paged_attn",
    )(page_table, *([cache_t] * pp), *([cache_lft] * pp), qbd_f, qbd_s, kvn_t, lft_new)


def _ffn_body(x_ref, ya_ref, yb_ref, yc_ref, woa_ref, wob_ref, woc_ref, g_ref, wg_ref, wu_ref, wd_ref,
              o_ref, h_scr, hn_scr, acc_scr):
    j = pl.program_id(1)

    @pl.when(j == 0)
    def _():
        h = (x_ref[...] + _dot(ya_ref[...].astype(BF16), woa_ref[...])
             + _dot(yb_ref[...].astype(BF16), wob_ref[...]) + _dot(yc_ref[...].astype(BF16), woc_ref[...]))
        h_scr[...] = h
        ms = jnp.mean(h * h, axis=-1, keepdims=True)
        hn_scr[...] = (h * lax.rsqrt(ms + RMS_EPS) * g_ref[...]).astype(BF16)
        acc_scr[...] = jnp.zeros(acc_scr.shape, F32)

    hn = hn_scr[...]
    gate = _dot(hn, wg_ref[...])
    up = _dot(hn, wu_ref[...])
    act = (gate * _sigmoid(gate) * up).astype(BF16)
    acc_scr[...] += _dot(act, wd_ref[...])

    @pl.when(j == pl.num_programs(1) - 1)
    def _():
        o_ref[...] = h_scr[...] + acc_scr[...]


def _ffn(x, ya, yb, yc, woa, wob, woc, g2, wg, wu, wd, tm, tf):
    n = x.shape[0]
    row = lambda a: pl.BlockSpec((tm, a.shape[1]), lambda i, j: (i, 0))
    full = lambda a: pl.BlockSpec(a.shape, lambda i, j: (0,) * a.ndim)
    return pl.pallas_call(
        _ffn_body,
        grid=(n // tm, D_FF // tf),
        in_specs=[row(x), row(ya), row(yb), row(yc), full(woa), full(wob), full(woc), full(g2),
                  pl.BlockSpec((D_MODEL, tf), lambda i, j: (0, j)),
                  pl.BlockSpec((D_MODEL, tf), lambda i, j: (0, j)),
                  pl.BlockSpec((tf, D_MODEL), lambda i, j: (j, 0))],
        out_specs=row(x),
        out_shape=jax.ShapeDtypeStruct((n, D_MODEL), F32),
        scratch_shapes=[pltpu.VMEM((tm, D_MODEL), F32), pltpu.VMEM((tm, D_MODEL), BF16),
                        pltpu.VMEM((tm, D_MODEL), F32)],
        compiler_params=_cparams(("arbitrary", "arbitrary")),
        name="merge_ffn",
    )(x, ya, yb, yc, woa, wob, woc, g2, wg, wu, wd)


def _head_slots(w):
    w = w.reshape(w.shape[:-1] + (H_ATT, 1, HEAD_DIM))
    half = (np.arange(H_ATT) % 2)[:, None] == np.arange(2)[None, :]
    w = jnp.where(jnp.asarray(half)[:, :, None], w, 0.0)
    return w.reshape(w.shape[:-3] + (QPAD,))


def _dense_heads(q):
    q = q.reshape(q.shape[:-1] + (H_ATT, 2, HEAD_DIM))
    half = (np.arange(H_ATT) % 2)[:, None] == np.arange(2)[None, :]
    return jnp.sum(jnp.where(jnp.asarray(half)[:, :, None], q, 0), axis=-2)


def _out_rows(w):
    w = w.reshape(H_ATT, 1, HEAD_DIM, w.shape[-1])
    half = ((np.arange(H_ATT) + 1) % 2)[:, None] == np.arange(2)[None, :]
    w = jnp.where(jnp.asarray(half)[:, :, None, None], w, 0.0)
    return w.reshape(QPAD, w.shape[-1])


def _block_diag(n, blk, val, limit=None):
    i = np.arange(n)
    m = (i[:, None] // blk == i[None, :] // blk).astype(np.float32) * val
    if limit is not None:
        m = m * ((i[:, None] < limit) & (i[None, :] < limit))
    return jnp.asarray(m, BF16)


def _pair_state(s):
    b = s.shape[0]
    s = s.reshape(b, N_PAIR, 2, HEAD_DIM, HEAD_DIM)
    return jnp.swapaxes(s, 2, 3).reshape(b, PAIR_ROWS, LANE)


def _unpair_state(s):
    b = s.shape[0]
    s = s.reshape(b, N_PAIR, HEAD_DIM, 2, HEAD_DIM)
    return jnp.swapaxes(s, 2, 3).reshape(b, H_RWKV, HEAD_DIM, HEAD_DIM)


def _pick(n, prefs):
    for p in prefs:
        if n % p == 0:
            return p
    return n


def kernel(x_prompt, x_sample, cache_kv, cache_logf, state_wkv, state_shift, page_table,
           norm1_g, w_in, shift_mu, rwkv_w0, rwkv_w2, rwkv_a0, rwkv_a2, rwkv_k_k, rwkv_k_a,
           rwkv_r_k, rwkv_ln_w, rwkv_ln_b, fox_b_f, fox_q_g, fox_k_g, w_out, norm2_g,
           ffn_w_gate, ffn_w_up, ffn_w_down):
    depth = w_in.shape[0]
    bp, tp, _ = x_prompt.shape
    db, ts, _ = x_sample.shape
    page = cache_kv.shape[2]

    o = N_SHIFT
    w_fq, w_fk, w_fv = (w_in[:, :, o + i * D_ATT:o + (i + 1) * D_ATT] for i in range(3))
    w_f = w_in[:, :, o + 3 * D_ATT:o + N_FOX_COLS]
    o = N_SHIFT + N_FOX_COLS
    w_sq, w_sk, w_sv = (w_in[:, :, o + i * D_ATT:o + (i + 1) * D_ATT] for i in range(3))
    w_proj = jnp.concatenate(
        [w_in[:, :, :N_SHIFT], w_fk, w_fv, w_sk, w_sv, _head_slots(w_fq), _head_slots(w_sq),
         jnp.pad(w_f, ((0, 0), (0, 0), (0, LANE - H_ATT)))], axis=-1).astype(BF16)
    bf_pad = jnp.pad(fox_b_f, ((0, 0), (0, LANE - H_ATT)))[:, None, :]
    kg_row = jnp.pad(jnp.tile(fox_k_g, (1, H_ATT)), ((0, 0), (0, 3 * LANE - D_ATT)))[:, None, :]
    qg_row = jnp.tile(jnp.tile(fox_q_g, (1, 2)), (1, H_ATT))[:, None, :]
    bdk = _block_diag(3 * LANE, HEAD_DIM, 1.0 / HEAD_DIM, limit=D_ATT)
    bdq = _block_diag(QPAD, LANE, 1.0 / HEAD_DIM)
    bd_sum = _block_diag(D_RWKV, HEAD_DIM, 1.0)
    bd_mean = _block_diag(D_RWKV, HEAD_DIM, 1.0 / HEAD_DIM)
    zl = jnp.zeros((depth, D_LORA, D_RWKV), F32)
    w2p = jnp.concatenate([rwkv_w2, zl], axis=1).astype(BF16)
    a2p = jnp.concatenate([zl, rwkv_a2], axis=1).astype(BF16)
    row3 = lambda a: a.reshape(depth, 1, -1)
    wo_a = w_out[:, :D_RWKV].astype(BF16)
    wo_b = w_out[:, D_RWKV:D_RWKV + D_ATT].astype(BF16)
    wo_c = w_out[:, D_RWKV + D_ATT:].astype(BF16)
    wo_b_slots, wo_c_slots = jax.vmap(_out_rows)(wo_b), jax.vmap(_out_rows)(wo_c)
    wg, wu, wd = ffn_w_gate.astype(BF16), ffn_w_up.astype(BF16), ffn_w_down.astype(BF16)

    tq = _pick(tp, (1024, 512, 256, 128))
    tk = _pick(tq, (512, 256, 128))
    tc_sum = _pick(tp, (256, 128))
    tri = jnp.asarray(np.tril(np.ones((tc_sum, tc_sum), np.float32)), BF16)
    sk = _pick(tk, (256, 128))
    u_sb = jnp.asarray(np.tile(np.tril(np.ones((sk, sk), np.float32), -1), (2, 1)), BF16)

    cache_t = jnp.transpose(cache_kv, (0, 1, 3, 4, 5, 2))
    cache_lft = jnp.pad(jnp.swapaxes(cache_logf, 2, 3), ((0, 0), (0, 0), (0, SUBLANE - H_ATT), (0, 0)))
    eye_h = jnp.asarray(np.eye(SUBLANE, H_ATT, dtype=np.float32), BF16)

    np_rows, ns_rows = bp * tp, db * ts
    tm_p = _pick(np_rows, (512, 256, 128))
    tm_ffn = _pick(np_rows, (1024, 512, 256, 128))
    tm_s = _pick(ns_rows, (256, 128, 64, 32, 16, 8))
    tc_scan = _pick(tp, (256, 128, 64, 32, 16, 8))
    nb_s = _pick(db, (4, 2, 1))
    pp = _pick(page_table.shape[1], (16, 8, 4, 2, 1))
    tf = _pick(D_FF, (256, 128))
    ts_pad = -(-ts // SUBLANE) * SUBLANE

    xp = x_prompt.reshape(np_rows, D_MODEL)
    xs = x_sample.reshape(ns_rows, D_MODEL)
    outs = [[] for _ in range(8)]

    def rwkv(feats, prev, s0p, l, nb, tc, tm):
        nseq, t, _ = feats.shape
        rows = nseq * t
        flat = feats.reshape(rows, N_SHIFT)
        if t % tm == 0:
            shifted, prev3, seq_len = flat, prev[:, None, :], t
        else:
            shifted = jnp.concatenate([prev[:, None, :], feats[:, :-1]], axis=1).reshape(rows, N_SHIFT)
            prev3, seq_len = jnp.zeros((1, 1, N_SHIFT), F32), None
        res = _prep(flat, shifted, prev3,
                    row3(shift_mu)[l], row3(rwkv_w0)[l], w2p[l], row3(rwkv_a0)[l], a2p[l],
                    row3(rwkv_k_k)[l], row3(rwkv_k_a)[l], row3(rwkv_r_k)[l], bd_sum, tm, seq_len)
        seq = [a.reshape(nseq, t, D_RWKV) for a in res[:6]]
        t_pad = -(-t // SUBLANE) * SUBLANE
        if t_pad != t:
            pad = lambda a, v: jnp.pad(a, ((0, 0), (0, t_pad - t), (0, 0)), constant_values=v)
            seq = [pad(a, 1.0 if i == 1 else 0.0) for i, a in enumerate(seq)]
        y, s_fin = _scan(*seq, s0p, nb, min(tc, t_pad))
        y = y[:, :t].reshape(rows, D_RWKV)
        ya = _post(y, res[6], res[7], row3(rwkv_ln_w)[l], row3(rwkv_ln_b)[l], bd_mean, tm)
        return ya, s_fin

    for l in range(depth):
        proj_consts = (row3(norm1_g)[l], w_proj[l], bdk, bdq, kg_row[l], qg_row[l], bf_pad[l])
        ffn_tail = (row3(norm2_g)[l], wg[l], wu[l], wd[l])

        feats, kv, kvb, qf, qs, lf = _proj(xp, *proj_consts, tm_p)
        feats3 = feats.reshape(bp, tp, N_SHIFT)
        ya, s_fin = rwkv(feats3, jnp.zeros((bp, N_SHIFT), F32), jnp.zeros((bp, PAIR_ROWS, LANE), F32),
                         l, bp, tc_scan, tm_p)
        cum, cumt = _cumsum(lf.reshape(bp, tp, LANE), tri)
        kvb3 = kvb.reshape(bp, tp, KV_W)
        yb = _fox(qf.reshape(bp, tp, QPAD), kvb3, cum, cumt, tq, tk)
        yc = _sb(qs.reshape(bp, tp, QPAD), kvb3, u_sb, tq, tk)
        xp = _ffn(xp, ya, yb.reshape(np_rows, QPAD), yc.reshape(np_rows, QPAD),
                  wo_a[l], wo_b_slots[l], wo_c_slots[l], *ffn_tail, tm_ffn, tf)
        outs[0].append(kv.reshape(bp, tp, 4, H_ATT, HEAD_DIM))
        outs[1].append(lf[:, :H_ATT].reshape(bp, tp, H_ATT))
        outs[2].append(_unpair_state(s_fin))
        outs[3].append(feats3[:, -1])

        feats, kv, kvb, qf, qs, lf = _proj(xs, *proj_consts, tm_s)
        feats3 = feats.reshape(db, ts, N_SHIFT)
        ya, s_fin = rwkv(feats3, state_shift[l], _pair_state(state_wkv[l]), l, nb_s, ts_pad, tm_s)

        def score_rows(q):
            q = _dense_heads(q.reshape(db, ts, QPAD))
            q = q[:, :, None, :, :] * eye_h[None, None, :, :, None]
            return q.reshape(db, ts * SUBLANE, D_ATT)

        kvn_t = jnp.pad(jnp.transpose(kvb.reshape(db, ts, 4, D_ATT), (0, 2, 3, 1)),
                        ((0, 0), (0, 0), (0, 0), (0, page - ts)))
        lft_new = jnp.pad(jnp.swapaxes(lf.reshape(db, ts, LANE)[:, :, :SUBLANE], 1, 2),
                          ((0, 0), (0, 0), (0, page - ts)))
        yb, yc = _paged(page_table, cache_t, l, cache_lft, score_rows(qf), score_rows(qs), kvn_t, lft_new, pp)
        xs = _ffn(xs, ya, yb.reshape(ns_rows, D_ATT), yc.reshape(ns_rows, D_ATT),
                  wo_a[l], wo_b[l], wo_c[l], *ffn_tail, tm_s, tf)
        outs[4].append(kv.reshape(db, ts, 4, H_ATT, HEAD_DIM))
        outs[5].append(lf[:, :H_ATT].reshape(db, ts, H_ATT))
        outs[6].append(_unpair_state(s_fin))
        outs[7].append(feats3[:, -1])

    return (xp.reshape(bp, tp, D_MODEL), xs.reshape(db, ts, D_MODEL)) + tuple(jnp.stack(o) for o in outs)
```

```python
import functools

import numpy as np
import jax
import jax.numpy as jnp
from jax import lax
from jax.experimental import pallas as pl
from jax.experimental.pallas import tpu as pltpu

F32 = jnp.float32
BF16 = jnp.bfloat16

D_MODEL = 1024
HEAD_DIM = 64
H_RWKV = 6
H_ATT = 5
D_RWKV = H_RWKV * HEAD_DIM
D_ATT = H_ATT * HEAD_DIM
D_LORA = 64
N_SHIFT = 4 * D_RWKV + 2 * D_LORA
N_FOX_COLS = 3 * D_ATT + H_ATT
D_FF = 2816
KV_W = 4 * D_ATT
RMS_EPS = 1e-6
GN_EPS = 64e-5
ATTN_SCALE = HEAD_DIM ** -0.5

LANE = 128
SUBLANE = 8
QPAD = H_ATT * LANE
N_PAIR = H_RWKV // 2
PAIR_ROWS = N_PAIR * HEAD_DIM

C_FEATS = 0
C_KV = N_SHIFT
C_QF = C_KV + KV_W
C_QS = C_QF + QPAD
C_F = C_QS + QPAD
N_PROJ = C_F + LANE

VMEM_LIMIT = 56 * 1024 * 1024


def _cparams(sem):
    return pltpu.CompilerParams(dimension_semantics=sem, vmem_limit_bytes=VMEM_LIMIT)


def _dot(a, b):
    return jnp.dot(a, b, preferred_element_type=F32)


def _nt_dot(a, b):
    return lax.dot_general(a, b, (((1,), (1,)), ((), ())), preferred_element_type=F32)


def _split3(x):
    hi = x.astype(BF16)
    r1 = x - hi.astype(F32)
    mid = r1.astype(BF16)
    lo = (r1 - mid.astype(F32)).astype(BF16)
    return hi, mid, lo


def _dot3_right(x, m):
    hi, mid, lo = _split3(x)
    return _dot(hi, m) + _dot(mid, m) + _dot(lo, m)


def _dot3_left(m, x):
    hi, mid, lo = _split3(x)
    return _dot(m, hi) + _dot(m, mid) + _dot(m, lo)


def _split2(x):
    hi = x.astype(BF16)
    return hi, (x - hi.astype(F32)).astype(BF16)


def _dot2_right(x, m):
    hi, lo = _split2(x)
    return _dot(hi, m) + _dot(lo, m)


def _softplus(x):
    return jnp.maximum(x, 0.0) + jnp.log1p(jnp.exp(-jnp.abs(x)))


def _sigmoid(x):
    return 1.0 / (1.0 + jnp.exp(-x))


def _proj_body(x_ref, g_ref, w_ref, bdk_ref, bdq_ref, kg_ref, qg_ref, bf_ref,
               feats_ref, kv_ref, kvb_ref, qf_ref, qs_ref, lf_ref):
    x = x_ref[...]
    ms = jnp.mean(x * x, axis=-1, keepdims=True)
    xn = (x * lax.rsqrt(ms + RMS_EPS) * g_ref[...]).astype(BF16)

    feats_ref[...] = _dot(xn, w_ref[:, C_FEATS:C_KV])

    ukv = _dot(xn, w_ref[:, C_KV:C_QF])
    ufk = ukv[:, :3 * LANE]
    msk = _dot3_right(ufk * ufk, bdk_ref[...])
    col = lax.broadcasted_iota(jnp.int32, ufk.shape, 1)
    kv0 = jnp.where(col < D_ATT, ufk * lax.rsqrt(msk + RMS_EPS) * kg_ref[...], ufk)
    kv_ref[:, :3 * LANE] = kv0
    kv_ref[:, 3 * LANE:] = ukv[:, 3 * LANE:]
    kvb_ref[:, :3 * LANE] = kv0.astype(BF16)
    kvb_ref[:, 3 * LANE:] = ukv[:, 3 * LANE:].astype(BF16)

    uqf = _dot(xn, w_ref[:, C_QF:C_QS])
    msq = _dot3_right(uqf * uqf, bdq_ref[...])
    qf_ref[...] = (uqf * lax.rsqrt(msq + RMS_EPS) * qg_ref[...] * ATTN_SCALE).astype(BF16)

    uqs = _dot(xn, w_ref[:, C_QS:C_F])
    qs_ref[...] = (uqs * ATTN_SCALE).astype(BF16)

    uf = _dot(xn, w_ref[:, C_F:N_PROJ]) + bf_ref[...]
    lf_ref[...] = -_softplus(-uf)


def _proj(x, g, w, bdk, bdq, kg, qg, bf, tm):
    n = x.shape[0]
    full = lambda a: pl.BlockSpec(a.shape, lambda i: (0,) * a.ndim)
    row = lambda c: pl.BlockSpec((tm, c), lambda i: (i, 0))
    return pl.pallas_call(
        _proj_body,
        grid=(n // tm,),
        in_specs=[row(D_MODEL), full(g), full(w), full(bdk), full(bdq), full(kg), full(qg), full(bf)],
        out_specs=[row(N_SHIFT), row(KV_W), row(KV_W), row(QPAD), row(QPAD), row(LANE)],
        out_shape=[jax.ShapeDtypeStruct((n, N_SHIFT), F32),
                   jax.ShapeDtypeStruct((n, KV_W), F32),
                   jax.ShapeDtypeStruct((n, KV_W), BF16),
                   jax.ShapeDtypeStruct((n, QPAD), BF16),
                   jax.ShapeDtypeStruct((n, QPAD), BF16),
                   jax.ShapeDtypeStruct((n, LANE), F32)],
        compiler_params=_cparams(("arbitrary",)),
        name="proj",
    )(x, g, w, bdk, bdq, kg, qg, bf)


def _prep_body(f_ref, sh_ref, prev_ref, mu_ref, w0_ref, w2_ref, a0_ref, a2_ref, kk_ref, ka_ref, rk_ref, bd_ref,
               nkk_ref, w_ref, kka_ref, k_ref, r_ref, v_ref, bonus_ref, gate_ref, *, tm, seq_len):
    f = f_ref[...]
    if seq_len is None:
        shifted = sh_ref[...]
    else:
        starts_sequence = (pl.program_id(0) * tm) % seq_len == 0
        first = jnp.where(starts_sequence, prev_ref[0], sh_ref[SUBLANE - 1:SUBLANE, :])
        row = lax.broadcasted_iota(jnp.int32, f.shape, 0)
        shifted = jnp.where(row == 0, first, pltpu.roll(f, 1, 0))
    m = f + (shifted - f) * mu_ref[...]
    r = m[:, 0:D_RWKV]
    k = m[:, D_RWKV:2 * D_RWKV]
    v = m[:, 2 * D_RWKV:3 * D_RWKV]
    g = m[:, 3 * D_RWKV:4 * D_RWKV]
    lo = m[:, 4 * D_RWKV:N_SHIFT]
    lw = _dot(jnp.tanh(lo).astype(BF16), w2_ref[...])
    la = _dot(lo.astype(BF16), a2_ref[...])
    w = -_softplus(-(w0_ref[...] + lw)) - 0.5
    decay = jnp.exp(-jnp.exp(w))
    a = _sigmoid(a0_ref[...] + la)
    kk = k * kk_ref[...]
    ss = _dot3_right(kk * kk, bd_ref[...])
    kk = kk / jnp.maximum(jnp.sqrt(ss), 1e-12)
    k2 = k * (1.0 + (a - 1.0) * ka_ref[...])
    rks = _dot3_right(r * k2 * rk_ref[...], bd_ref[...])
    nkk_ref[...] = -kk
    w_ref[...] = decay
    kka_ref[...] = kk * a
    k_ref[...] = k2
    r_ref[...] = r
    v_ref[...] = v
    bonus_ref[...] = rks * v
    gate_ref[...] = _sigmoid(g)


def _prep(feats, shifted, prev, mu, w0, w2p, a0, a2p, k_k, k_a, r_k, bd, tm, seq_len):
    n = feats.shape[0]
    full = lambda a: pl.BlockSpec(a.shape, lambda i: (0,) * a.ndim)
    blk = lambda c: pl.BlockSpec((tm, c), lambda i: (i, 0))
    if seq_len is None:
        sh_spec = blk(N_SHIFT)
        prev_spec = pl.BlockSpec((1, 1, N_SHIFT), lambda i: (0, 0, 0))
    else:
        sh_spec = pl.BlockSpec((SUBLANE, N_SHIFT), lambda i: (jnp.maximum(i * (tm // SUBLANE) - 1, 0), 0))
        prev_spec = pl.BlockSpec((1, 1, N_SHIFT), lambda i: ((i * tm) // seq_len, 0, 0))
    o384 = jax.ShapeDtypeStruct((n, D_RWKV), F32)
    return pl.pallas_call(
        functools.partial(_prep_body, tm=tm, seq_len=seq_len),
        grid=(n // tm,),
        in_specs=[blk(N_SHIFT), sh_spec, prev_spec]
        + [full(a) for a in (mu, w0, w2p, a0, a2p, k_k, k_a, r_k, bd)],
        out_specs=[blk(D_RWKV)] * 8,
        out_shape=[o384] * 8,
        compiler_params=_cparams(("arbitrary",)),
        name="rwkv_prep",
    )(feats, shifted, prev, mu, w0, w2p, a0, a2p, k_k, k_a, r_k, bd)


def _scan_body(nkk_ref, w_ref, kka_ref, k_ref, r_ref, v_ref, s0_ref, ones_ref, bv_ref,
               y_ref, sT_ref, s_scr, *, nb, tc):
    c = pl.program_id(1)

    @pl.when(c == 0)
    def _():
        s_scr[...] = s0_ref[...]

    ones2 = ones_ref[...]
    zpad = jnp.zeros((LANE - 3 * SUBLANE, LANE), F32)
    ztile = jnp.zeros((HEAD_DIM, LANE), F32)
    lane_id = lax.broadcasted_iota(jnp.int32, (PAIR_ROWS, LANE), 1)
    lane_in_half = lane_id % HEAD_DIM
    left = lane_id < HEAD_DIM

    def hilo(z):
        return jnp.concatenate(_split2(z), axis=1)

    def bc(x, u):
        return jnp.concatenate(
            [jnp.broadcast_to(x[u:u + 1, j * LANE:(j + 1) * LANE], (HEAD_DIM, LANE))
             for j in range(N_PAIR)], axis=0)

    def group(i, carry):
        t0 = pl.multiple_of(i * SUBLANE, SUBLANE)
        lhs = []
        for b in range(nb):
            vr = v_ref[b, pl.ds(t0, SUBLANE), :]
            for j in range(N_PAIR):
                hi, mid, lo = _split3(vr[:, j * LANE:(j + 1) * LANE])
                stacked = jnp.concatenate([hi.astype(F32), mid.astype(F32), lo.astype(F32), zpad], axis=0)
                at = stacked.T.astype(BF16)
                lhs.append(jnp.concatenate([at[:HEAD_DIM], at[HEAD_DIM:]], axis=1))
        vt_all = _dot(jnp.concatenate(lhs, axis=0), bv_ref[...])

        rows = [[ref[b, pl.ds(t0, SUBLANE), :] for ref in (nkk_ref, w_ref, kka_ref, k_ref, r_ref)]
                for b in range(nb)]
        s = [s_scr[b] for b in range(nb)]
        g_acc = [jnp.zeros((PAIR_ROWS, LANE), F32) for _ in range(nb)]
        for u in range(SUBLANE):
            for b in range(nb):
                nkk, w, kka, k, r = rows[b]
                z = s[b] * bc(nkk, u)
                sa = jnp.where(left,
                               jnp.sum(jnp.where(left, z, 0.0), axis=1, keepdims=True),
                               jnp.sum(jnp.where(left, 0.0, z), axis=1, keepdims=True))
                vt = vt_all[b * PAIR_ROWS:(b + 1) * PAIR_ROWS, u * LANE:(u + 1) * LANE]
                s[b] = s[b] * bc(w, u) + sa * bc(kka, u) + vt * bc(k, u)
                yt = _dot(hilo(s[b] * bc(r, u)), ones2)
                g_acc[b] = jnp.where(lane_in_half == u, yt, g_acc[b])
        for b in range(nb):
            s_scr[b] = s[b]
            for j in range(N_PAIR):
                gj = g_acc[b][j * HEAD_DIM:(j + 1) * HEAD_DIM]
                tt = jnp.concatenate([gj, ztile], axis=0).T
                y_ref[b, pl.ds(t0, SUBLANE), j * LANE:(j + 1) * LANE] = (
                    tt[:SUBLANE] + pltpu.roll(tt[HEAD_DIM:HEAD_DIM + SUBLANE], HEAD_DIM, 1))
        return carry

    lax.fori_loop(0, tc // SUBLANE, group, 0)

    @pl.when(c == pl.num_programs(1) - 1)
    def _():
        sT_ref[...] = s_scr[...]


def _scan_consts():
    k = np.arange(2 * LANE)
    seg_k = (k % LANE) // HEAD_DIM
    lane = np.arange(LANE)
    ones2 = (seg_k[:, None] == lane[None, :] // HEAD_DIM)
    cc = k % LANE
    col = np.arange(SUBLANE * LANE)
    bv = ((cc[:, None] < 3 * SUBLANE) & (cc[:, None] % SUBLANE == col[None, :] // LANE)
          & ((k[:, None] // LANE) == (col[None, :] % LANE) // HEAD_DIM))
    as_bf = lambda a: jnp.asarray(a.astype(np.float32), BF16)
    return as_bf(ones2), as_bf(bv)


def _scan(nkk, w, kka, k, r, v, s0, nb, tc):
    b, t, _ = r.shape
    ones2, bv = _scan_consts()
    blk = pl.BlockSpec((nb, tc, D_RWKV), lambda i, c: (i, c, 0))
    sblk = pl.BlockSpec((nb, PAIR_ROWS, LANE), lambda i, c: (i, 0, 0))
    full = lambda a: pl.BlockSpec(a.shape, lambda i, c: (0,) * a.ndim)
    return pl.pallas_call(
        functools.partial(_scan_body, nb=nb, tc=tc),
        grid=(b // nb, t // tc),
        in_specs=[blk] * 6 + [sblk, full(ones2), full(bv)],
        out_specs=[blk, sblk],
        out_shape=[jax.ShapeDtypeStruct((b, t, D_RWKV), F32),
                   jax.ShapeDtypeStruct((b, PAIR_ROWS, LANE), F32)],
        scratch_shapes=[pltpu.VMEM((nb, PAIR_ROWS, LANE), F32)],
        compiler_params=_cparams(("arbitrary", "arbitrary")),
        name="rwkv_scan",
    )(nkk, w, kka, k, r, v, s0, ones2, bv)


def _post_body(y_ref, bonus_ref, gate_ref, lnw_ref, lnb_ref, bd_ref, o_ref):
    y = y_ref[...]
    mean = _dot3_right(y, bd_ref[...])
    d = y - mean
    var = _dot3_right(d * d, bd_ref[...])
    yn = d * lax.rsqrt(var + GN_EPS) * lnw_ref[...] + lnb_ref[...]
    o_ref[...] = ((yn + bonus_ref[...]) * gate_ref[...]).astype(BF16)


def _post(y, bonus, gate, lnw, lnb, bdm, tm):
    n = y.shape[0]
    full = lambda a: pl.BlockSpec(a.shape, lambda i: (0,) * a.ndim)
    blk = pl.BlockSpec((tm, D_RWKV), lambda i: (i, 0))
    return pl.pallas_call(
        _post_body,
        grid=(n // tm,),
        in_specs=[blk, blk, blk, full(lnw), full(lnb), full(bdm)],
        out_specs=blk,
        out_shape=jax.ShapeDtypeStruct((n, D_RWKV), BF16),
        compiler_params=_cparams(("arbitrary",)),
        name="rwkv_post",
    )(y, bonus, gate, lnw, lnb, bdm)


def _cumsum_body(x_ref, tri_ref, cum_ref, cumt_ref, carry):
    @pl.when(pl.program_id(1) == 0)
    def _():
        carry[...] = jnp.zeros(carry.shape, F32)

    c = _dot3_left(tri_ref[...], x_ref[0]) + carry[...]
    cum_ref[0] = c
    cumt_ref[0] = c.T[0:SUBLANE, :]
    n = c.shape[0]
    carry[...] = c[n - 1:n, :]


def _cumsum(lf, tri):
    b, t, _ = lf.shape
    tc = tri.shape[0]
    return pl.pallas_call(
        _cumsum_body,
        grid=(b, t // tc),
        in_specs=[pl.BlockSpec((1, tc, LANE), lambda i, j: (i, j, 0)),
                  pl.BlockSpec(tri.shape, lambda i, j: (0, 0))],
        out_specs=[pl.BlockSpec((1, tc, LANE), lambda i, j: (i, j, 0)),
                   pl.BlockSpec((1, SUBLANE, tc), lambda i, j: (i, 0, j))],
        out_shape=[jax.ShapeDtypeStruct((b, t, LANE), F32), jax.ShapeDtypeStruct((b, SUBLANE, t), F32)],
        scratch_shapes=[pltpu.VMEM((1, LANE), F32)],
        compiler_params=_cparams(("arbitrary", "arbitrary")),
        name="logf_cumsum",
    )(lf, tri)


def _fox_body(q_ref, k_ref, v_ref, cum_ref, cumt_ref, o_ref, *, tq, tk):
    h = pl.program_id(1)
    qi = pl.program_id(2)
    nd = tq // tk
    lane = lax.broadcasted_iota(jnp.int32, (tq, LANE), 1)
    cq = jnp.sum(jnp.where(lane == h, cum_ref[0], 0.0), axis=1, keepdims=True)
    sub = lax.broadcasted_iota(jnp.int32, (SUBLANE, tk), 0)
    q = q_ref[0]
    on_or_below = (lax.broadcasted_iota(jnp.int32, (tk, tk), 1)
                   <= lax.broadcasted_iota(jnp.int32, (tk, tk), 0))

    def block(j, carry, q_rows, cq_rows, diagonal):
        k0 = pl.multiple_of(j * tk, tk)
        kw = k_ref[0, pl.ds(k0, tk), :]
        vw = v_ref[0, pl.ds(k0, tk), :]
        ck = jnp.sum(jnp.where(sub == h, cumt_ref[0, :, pl.ds(k0, tk)], 0.0), axis=0, keepdims=True)
        m_old, l_old, acc = carry
        s = _nt_dot(q_rows, kw) + cq_rows - ck
        if diagonal:
            s = jnp.where(on_or_below, s, -jnp.inf)
        m_new = jnp.maximum(m_old, jnp.max(s, axis=1, keepdims=True))
        alpha = jnp.exp(m_old - m_new)
        pr = jnp.exp(s - m_new)
        l_new = alpha * l_old + jnp.sum(pr, axis=1, keepdims=True)
        return m_new, l_new, alpha * acc + _dot(pr.astype(BF16), vw)

    carry = (jnp.full((tq, 1), -jnp.inf, F32), jnp.zeros((tq, 1), F32), jnp.zeros((tq, LANE), F32))
    carry = lax.fori_loop(0, qi * nd, lambda j, c: block(j, c, q, cq, False), carry)
    lane_o = lax.broadcasted_iota(jnp.int32, (tk, LANE), 1)
    valid = (lane_o >= HEAD_DIM) == ((h + 1) % 2 == 1)
    for g in range(nd):
        rows = slice(g * tk, (g + 1) * tk)
        cg = tuple(x[rows] for x in carry)
        for jj in range(g):
            cg = block(qi * nd + jj, cg, q[rows], cq[rows], False)
        cg = block(qi * nd + g, cg, q[rows], cq[rows], True)
        o_ref[0, rows, :] = jnp.where(valid, cg[2] / cg[1], 0.0).astype(BF16)


def _sb_body(q_ref, k_ref, v_ref, u_ref, o_ref, *, tq, tk):
    h = pl.program_id(1)
    qi = pl.program_id(2)
    nd = tq // tk
    sk = u_ref.shape[1]
    q = q_ref[0]
    below = (lax.broadcasted_iota(jnp.int32, (tk, tk), 1) < lax.broadcasted_iota(jnp.int32, (tk, tk), 0))

    def block(j, carry, q_rows, masked):
        k0 = pl.multiple_of(j * tk, tk)
        kw = k_ref[0, pl.ds(k0, tk), :]
        vw = v_ref[0, pl.ds(k0, tk), :]
        c_run, acc = carry
        z = _nt_dot(q_rows, kw)
        x = jnp.minimum(-z, 0.0) - jnp.log(1.0 + jnp.exp(-jnp.abs(z)))
        zx = x + z
        if masked:
            mask = below
            x = jnp.where(mask, x, 0.0)
        ws = []
        for sb in reversed(range(tk // sk)):
            xs = x[:, sb * sk:(sb + 1) * sk]
            suf = _dot(jnp.concatenate(_split2(xs), axis=1), u_ref[...]) + c_run
            ws.append(jnp.exp(zx[:, sb * sk:(sb + 1) * sk] + suf))
            c_run = suf[:, 0:1] + xs[:, 0:1]
        w = jnp.concatenate(ws[::-1], axis=1)
        if masked:
            w = jnp.where(mask, w, 0.0)
        return c_run, acc + _dot(w.astype(BF16), vw)

    groups = []
    for g in range(nd):
        rows = slice(g * tk, (g + 1) * tk)
        cg = block(qi * nd + g, (jnp.zeros((tk, 1), F32), jnp.zeros((tk, LANE), F32)), q[rows], True)
        for jj in reversed(range(g)):
            cg = block(qi * nd + jj, cg, q[rows], False)
        groups.append(cg)
    carry = tuple(jnp.concatenate([cg[i] for cg in groups], axis=0) for i in range(2))
    carry = lax.fori_loop(0, qi * nd, lambda i, c: block(qi * nd - 1 - i, c, q, False), carry)
    lane_o = lax.broadcasted_iota(jnp.int32, (tq, LANE), 1)
    valid = (lane_o >= HEAD_DIM) == ((h + 1) % 2 == 1)
    o_ref[0] = jnp.where(valid, carry[1], 0.0).astype(BF16)


def _attn_specs(tq, t, group):
    g0 = group * (2 * D_ATT // LANE)
    return ([pl.BlockSpec((1, tq, LANE), lambda b, h, i: (b, i, h)),
             pl.BlockSpec((1, t, LANE), lambda b, h, i: (b, 0, g0 + h // 2)),
             pl.BlockSpec((1, t, LANE), lambda b, h, i: (b, 0, g0 + (H_ATT + h) // 2))],
            pl.BlockSpec((1, tq, LANE), lambda b, h, i: (b, i, h)))


def _fox(q, kvb, cum, cumt, tq, tk):
    b, t, _ = q.shape
    ins, out = _attn_specs(tq, t, 0)
    return pl.pallas_call(
        functools.partial(_fox_body, tq=tq, tk=tk),
        grid=(b, H_ATT, t // tq),
        in_specs=ins + [pl.BlockSpec((1, tq, LANE), lambda b, h, i: (b, i, 0)),
                        pl.BlockSpec((1, SUBLANE, t), lambda b, h, i: (b, 0, 0))],
        out_specs=out,
        out_shape=jax.ShapeDtypeStruct((b, t, QPAD), BF16),
        compiler_params=_cparams(("arbitrary", "arbitrary", "arbitrary")),
        name="fox_attn",
    )(q, kvb, kvb, cum, cumt)


def _sb(q, kvb, u, tq, tk):
    b, t, _ = q.shape
    ins, out = _attn_specs(tq, t, 1)
    return pl.pallas_call(
        functools.partial(_sb_body, tq=tq, tk=tk),
        grid=(b, H_ATT, t // tq),
        in_specs=ins + [pl.BlockSpec(u.shape, lambda b, h, i: (0, 0))],
        out_specs=out,
        out_shape=jax.ShapeDtypeStruct((b, t, QPAD), BF16),
        compiler_params=_cparams(("arbitrary", "arbitrary", "arbitrary")),
        name="sb_attn",
    )(q, kvb, kvb, u)


def _paged_body(pt_ref, *refs, pp, page, n_new):
    pages = refs[:pp]
    lfts = refs[pp:2 * pp]
    qf_ref, qs_ref, kvn_ref, lfn_ref, yb_ref, yc_ref = refs[2 * pp:2 * pp + 6]
    m_scr, l_scr, accf_scr, accs_scr, cf_scr, cs_scr, cq_scr = refs[2 * pp + 6:]
    g = pl.program_id(1)
    nrow = SUBLANE * n_new

    ji = lax.broadcasted_iota(jnp.int32, (page, page), 0)
    si = lax.broadcasted_iota(jnp.int32, (page, page), 1)
    u_excl = (ji > si).astype(BF16)
    u_incl = (ji <= si).astype(BF16)
    key = lax.broadcasted_iota(jnp.int32, (nrow, page), 1)
    qrow = lax.broadcasted_iota(jnp.int32, (nrow, page), 0) // SUBLANE

    def attend(blocks, biases, mask_f, mask_s):
        n = len(blocks)
        sf = [_dot(qf_ref[0], blk[0]) + cq_scr[...] + jnp.concatenate([bias] * n_new, axis=0)
              for blk, bias in zip(blocks, biases)]
        if mask_f is not None:
            sf = [jnp.where(mask_f, s, -jnp.inf) for s in sf]
        m_old = m_scr[...]
        m_new = jnp.maximum(m_old, jnp.max(functools.reduce(jnp.maximum, sf), axis=1, keepdims=True))
        alpha = jnp.exp(m_old - m_new)
        prs = [jnp.exp(s - m_new) for s in sf]
        l_scr[...] = alpha * l_scr[...] + jnp.sum(functools.reduce(jnp.add, prs), axis=1, keepdims=True)
        pv = functools.reduce(jnp.add, [_nt_dot(p.astype(BF16), blk[1]) for p, blk in zip(prs, blocks)])
        accf_scr[...] = alpha * accf_scr[...] + pv
        m_scr[...] = m_new

        zs = [_dot(qs_ref[0], blk[2]) for blk in blocks]
        xs = [-_softplus(z) for z in zs]
        if mask_s is not None:
            xs = [jnp.where(mask_s, x, 0.0) for x in xs]
        sufs = [_dot3_right(x, u_excl) for x in xs]
        c = cs_scr[...]
        av = None
        for z, x, suf, blk in zip(zs, xs, sufs, blocks):
            a = jnp.exp(z + x + suf + c)
            if mask_s is not None:
                a = jnp.where(mask_s, a, 0.0)
            d = _nt_dot(a.astype(BF16), blk[3])
            av = d if av is None else av + d
            c = c + suf[:, 0:1] + x[:, 0:1]
        accs_scr[...] = accs_scr[...] + av
        cs_scr[...] = c

    @pl.when(g == 0)
    def _():
        m_scr[...] = jnp.full(m_scr.shape, -jnp.inf, F32)
        l_scr[...] = jnp.zeros(l_scr.shape, F32)
        accf_scr[...] = jnp.zeros(accf_scr.shape, F32)
        accs_scr[...] = jnp.zeros(accs_scr.shape, F32)
        cf_scr[...] = jnp.zeros(cf_scr.shape, F32)
        cs_scr[...] = jnp.zeros(cs_scr.shape, F32)
        pn = _dot3_right(lfn_ref[0], u_incl)
        for i in range(n_new):
            cq_scr[SUBLANE * i:SUBLANE * (i + 1), :] = jnp.broadcast_to(pn[:, i:i + 1], (SUBLANE, page))
        attend([tuple(kvn_ref[0, sl] for sl in range(4))], [-pn], key <= qrow, key < qrow)

    blocks, biases = [], []
    cf = cf_scr[...]
    for i in range(pp):
        blocks.append(tuple(pages[i][0, sl].reshape(D_ATT, page).astype(BF16) for sl in range(4)))
        lf = lfts[i][0]
        suf = _dot3_right(lf, u_excl)
        biases.append(suf + cf)
        cf = cf + suf[:, 0:1] + lf[:, 0:1]
    cf_scr[...] = cf
    attend(blocks, biases, None, None)

    @pl.when(g == pl.num_programs(1) - 1)
    def _():
        hrow = lax.broadcasted_iota(jnp.int32, (SUBLANE, D_ATT), 0)
        hcol = lax.broadcasted_iota(jnp.int32, (SUBLANE, D_ATT), 1) // HEAD_DIM
        own = hrow == hcol
        inv_l = 1.0 / l_scr[...]
        for i in range(n_new):
            rs = slice(SUBLANE * i, SUBLANE * (i + 1))
            yf = accf_scr[rs, :] * inv_l[rs]
            yb_ref[0, i:i + 1, :] = jnp.sum(jnp.where(own, yf, 0.0), axis=0, keepdims=True)
            yc_ref[0, i:i + 1, :] = jnp.sum(jnp.where(own, accs_scr[rs, :], 0.0), axis=0, keepdims=True)


def _paged(page_table, cache_t, layer, cache_lft, qbd_f, qbd_s, kvn_t, lft_new, pp):
    db, n_pages = page_table.shape
    page = cache_t.shape[-1]
    n_new = qbd_f.shape[1] // SUBLANE
    nrow = SUBLANE * n_new
    ngrp = n_pages // pp

    def page_map(i, nd):
        return lambda b, g, pt: (layer, pt[b, n_pages - 1 - (g * pp + i)]) + (0,) * nd

    per_seq = lambda a: pl.BlockSpec((1,) + a.shape[1:], lambda b, g, pt: (b,) + (0,) * (a.ndim - 1))
    grid_spec = pltpu.PrefetchScalarGridSpec(
        num_scalar_prefetch=1,
        grid=(db, ngrp),
        in_specs=([pl.BlockSpec((1, 1, 4, H_ATT, HEAD_DIM, page), page_map(i, 4)) for i in range(pp)]
                  + [pl.BlockSpec((1, 1, SUBLANE, page), page_map(i, 2)) for i in range(pp)]
                  + [per_seq(qbd_f), per_seq(qbd_s), per_seq(kvn_t), per_seq(lft_new)]),
        out_specs=[pl.BlockSpec((1, n_new, D_ATT), lambda b, g, pt: (b, 0, 0))] * 2,
        scratch_shapes=[pltpu.VMEM((nrow, 1), F32), pltpu.VMEM((nrow, 1), F32),
                        pltpu.VMEM((nrow, D_ATT), F32), pltpu.VMEM((nrow, D_ATT), F32),
                        pltpu.VMEM((SUBLANE, 1), F32), pltpu.VMEM((nrow, 1), F32),
                        pltpu.VMEM((nrow, page), F32)])
    o = jax.ShapeDtypeStruct((db, n_new, D_ATT), F32)

    def body(pt_ref, *refs):
        pages = [r.at[0] for r in refs[:pp]]
        lfts = [r.at[0] for r in refs[pp:2 * pp]]
        _paged_body(pt_ref, *pages, *lfts, *refs[2 * pp:], pp=pp, page=page, n_new=n_new)

    return pl.pallas_call(
        body,
        grid_spec=grid_spec,
        out_shape=[o, o],
        compiler_params=_cparams(("arbitrary", "arbitrary")),
        name="paged_attn",
    )(page_table, *([cache_t] * pp), *([cache_lft] * pp), qbd_f, qbd_s, kvn_t, lft_new)


def _ffn_body(x_ref, ya_ref, yb_ref, yc_ref, woa_ref, wob_ref, woc_ref, g_ref, wg_ref, wu_ref, wd_ref,
              o_ref, h_scr, hn_scr, acc_scr):
    j = pl.program_id(1)

    @pl.when(j == 0)
    def _():
        h = (x_ref[...] + _dot(ya_ref[...].astype(BF16), woa_ref[...])
             + _dot(yb_ref[...].astype(BF16), wob_ref[...]) + _dot(yc_ref[...].astype(BF16), woc_ref[...]))
        h_scr[...] = h
        ms = jnp.mean(h * h, axis=-1, keepdims=True)
        hn_scr[...] = (h * lax.rsqrt(ms + RMS_EPS) * g_ref[...]).astype(BF16)
        acc_scr[...] = jnp.zeros(acc_scr.shape, F32)

    hn = hn_scr[...]
    gate = _dot(hn, wg_ref[...])
    up = _dot(hn, wu_ref[...])
    act = (gate * _sigmoid(gate) * up).astype(BF16)
    acc_scr[...] += _dot(act, wd_ref[...])

    @pl.when(j == pl.num_programs(1) - 1)
    def _():
        o_ref[...] = h_scr[...] + acc_scr[...]


def _ffn(x, ya, yb, yc, woa, wob, woc, g2, wg, wu, wd, tm, tf):
    n = x.shape[0]
    row = lambda a: pl.BlockSpec((tm, a.shape[1]), lambda i, j: (i, 0))
    full = lambda a: pl.BlockSpec(a.shape, lambda i, j: (0,) * a.ndim)
    return pl.pallas_call(
        _ffn_body,
        grid=(n // tm, D_FF // tf),
        in_specs=[row(x), row(ya), row(yb), row(yc), full(woa), full(wob), full(woc), full(g2),
                  pl.BlockSpec((D_MODEL, tf), lambda i, j: (0, j)),
                  pl.BlockSpec((D_MODEL, tf), lambda i, j: (0, j)),
                  pl.BlockSpec((tf, D_MODEL), lambda i, j: (j, 0))],
        out_specs=row(x),
        out_shape=jax.ShapeDtypeStruct((n, D_MODEL), F32),
        scratch_shapes=[pltpu.VMEM((tm, D_MODEL), F32), pltpu.VMEM((tm, D_MODEL), BF16),
                        pltpu.VMEM((tm, D_MODEL), F32)],
        compiler_params=_cparams(("arbitrary", "arbitrary")),
        name="merge_ffn",
    )(x, ya, yb, yc, woa, wob, woc, g2, wg, wu, wd)


def _head_slots(w):
    w = w.reshape(w.shape[:-1] + (H_ATT, 1, HEAD_DIM))
    half = (np.arange(H_ATT) % 2)[:, None] == np.arange(2)[None, :]
    w = jnp.where(jnp.asarray(half)[:, :, None], w, 0.0)
    return w.reshape(w.shape[:-3] + (QPAD,))


def _dense_heads(q):
    q = q.reshape(q.shape[:-1] + (H_ATT, 2, HEAD_DIM))
    half = (np.arange(H_ATT) % 2)[:, None] == np.arange(2)[None, :]
    return jnp.sum(jnp.where(jnp.asarray(half)[:, :, None], q, 0), axis=-2)


def _out_rows(w):
    w = w.reshape(H_ATT, 1, HEAD_DIM, w.shape[-1])
    half = ((np.arange(H_ATT) + 1) % 2)[:, None] == np.arange(2)[None, :]
    w = jnp.where(jnp.asarray(half)[:, :, None, None], w, 0.0)
    return w.reshape(QPAD, w.shape[-1])


def _block_diag(n, blk, val, limit=None):
    i = np.arange(n)
    m = (i[:, None] // blk == i[None, :] // blk).astype(np.float32) * val
    if limit is not None:
        m = m * ((i[:, None] < limit) & (i[None, :] < limit))
    return jnp.asarray(m, BF16)


def _pair_state(s):
    b = s.shape[0]
    s = s.reshape(b, N_PAIR, 2, HEAD_DIM, HEAD_DIM)
    return jnp.swapaxes(s, 2, 3).reshape(b, PAIR_ROWS, LANE)


def _unpair_state(s):
    b = s.shape[0]
    s = s.reshape(b, N_PAIR, HEAD_DIM, 2, HEAD_DIM)
    return jnp.swapaxes(s, 2, 3).reshape(b, H_RWKV, HEAD_DIM, HEAD_DIM)


def _pick(n, prefs):
    for p in prefs:
        if n % p == 0:
            return p
    return n


def kernel(x_prompt, x_sample, cache_kv, cache_logf, state_wkv, state_shift, page_table,
           norm1_g, w_in, shift_mu, rwkv_w0, rwkv_w2, rwkv_a0, rwkv_a2, rwkv_k_k, rwkv_k_a,
           rwkv_r_k, rwkv_ln_w, rwkv_ln_b, fox_b_f, fox_q_g, fox_k_g, w_out, norm2_g,
           ffn_w_gate, ffn_w_up, ffn_w_down):
    depth = w_in.shape[0]
    bp, tp, _ = x_prompt.shape
    db, ts, _ = x_sample.shape
    page = cache_kv.shape[2]

    o = N_SHIFT
    w_fq, w_fk, w_fv = (w_in[:, :, o + i * D_ATT:o + (i + 1) * D_ATT] for i in range(3))
    w_f = w_in[:, :, o + 3 * D_ATT:o + N_FOX_COLS]
    o = N_SHIFT + N_FOX_COLS
    w_sq, w_sk, w_sv = (w_in[:, :, o + i * D_ATT:o + (i + 1) * D_ATT] for i in range(3))
    w_proj = jnp.concatenate(
        [w_in[:, :, :N_SHIFT], w_fk, w_fv, w_sk, w_sv, _head_slots(w_fq), _head_slots(w_sq),
         jnp.pad(w_f, ((0, 0), (0, 0), (0, LANE - H_ATT)))], axis=-1).astype(BF16)
    bf_pad = jnp.pad(fox_b_f, ((0, 0), (0, LANE - H_ATT)))[:, None, :]
    kg_row = jnp.pad(jnp.tile(fox_k_g, (1, H_ATT)), ((0, 0), (0, 3 * LANE - D_ATT)))[:, None, :]
    qg_row = jnp.tile(jnp.tile(fox_q_g, (1, 2)), (1, H_ATT))[:, None, :]
    bdk = _block_diag(3 * LANE, HEAD_DIM, 1.0 / HEAD_DIM, limit=D_ATT)
    bdq = _block_diag(QPAD, LANE, 1.0 / HEAD_DIM)
    bd_sum = _block_diag(D_RWKV, HEAD_DIM, 1.0)
    bd_mean = _block_diag(D_RWKV, HEAD_DIM, 1.0 / HEAD_DIM)
    zl = jnp.zeros((depth, D_LORA, D_RWKV), F32)
    w2p = jnp.concatenate([rwkv_w2, zl], axis=1).astype(BF16)
    a2p = jnp.concatenate([zl, rwkv_a2], axis=1).astype(BF16)
    row3 = lambda a: a.reshape(depth, 1, -1)
    wo_a = w_out[:, :D_RWKV].astype(BF16)
    wo_b = w_out[:, D_RWKV:D_RWKV + D_ATT].astype(BF16)
    wo_c = w_out[:, D_RWKV + D_ATT:].astype(BF16)
    wo_b_slots, wo_c_slots = jax.vmap(_out_rows)(wo_b), jax.vmap(_out_rows)(wo_c)
    wg, wu, wd = ffn_w_gate.astype(BF16), ffn_w_up.astype(BF16), ffn_w_down.astype(BF16)

    tq = _pick(tp, (1024, 512, 256, 128))
    tk = _pick(tq, (512, 256, 128))
    tc_sum = _pick(tp, (256, 128))
    tri = jnp.asarray(np.tril(np.ones((tc_sum, tc_sum), np.float32)), BF16)
    sk = _pick(tk, (256, 128))
    u_sb = jnp.asarray(np.tile(np.tril(np.ones((sk, sk), np.float32), -1), (2, 1)), BF16)

    cache_t = jnp.transpose(cache_kv, (0, 1, 3, 4, 5, 2))
    cache_lft = jnp.pad(jnp.swapaxes(cache_logf, 2, 3), ((0, 0), (0, 0), (0, SUBLANE - H_ATT), (0, 0)))
    eye_h = jnp.asarray(np.eye(SUBLANE, H_ATT, dtype=np.float32), BF16)

    np_rows, ns_rows = bp * tp, db * ts
    tm_p = _pick(np_rows, (512, 256, 128))
    tm_ffn = _pick(np_rows, (1024, 512, 256, 128))
    tm_s = _pick(ns_rows, (256, 128, 64, 32, 16, 8))
    tc_scan = _pick(tp, (256, 128, 64, 32, 16, 8))
    nb_s = _pick(db, (4, 2, 1))
    pp = _pick(page_table.shape[1], (16, 8, 4, 2, 1))
    tf = _pick(D_FF, (256, 128))
    ts_pad = -(-ts // SUBLANE) * SUBLANE

    xp = x_prompt.reshape(np_rows, D_MODEL)
    xs = x_sample.reshape(ns_rows, D_MODEL)
    outs = [[] for _ in range(8)]

    def rwkv(feats, prev, s0p, l, nb, tc, tm):
        nseq, t, _ = feats.shape
        rows = nseq * t
        flat = feats.reshape(rows, N_SHIFT)
        if t % tm == 0:
            shifted, prev3, seq_len = flat, prev[:, None, :], t
        else:
            shifted = jnp.concatenate([prev[:, None, :], feats[:, :-1]], axis=1).reshape(rows, N_SHIFT)
            prev3, seq_len = jnp.zeros((1, 1, N_SHIFT), F32), None
        res = _prep(flat, shifted, prev3,
                    row3(shift_mu)[l], row3(rwkv_w0)[l], w2p[l], row3(rwkv_a0)[l], a2p[l],
                    row3(rwkv_k_k)[l], row3(rwkv_k_a)[l], row3(rwkv_r_k)[l], bd_sum, tm, seq_len)
        seq = [a.reshape(nseq, t, D_RWKV) for a in res[:6]]
        t_pad = -(-t // SUBLANE) * SUBLANE
        if t_pad != t:
            pad = lambda a, v: jnp.pad(a, ((0, 0), (0, t_pad - t), (0, 0)), constant_values=v)
            seq = [pad(a, 1.0 if i == 1 else 0.0) for i, a in enumerate(seq)]
        y, s_fin = _scan(*seq, s0p, nb, min(tc, t_pad))
        y = y[:, :t].reshape(rows, D_RWKV)
        ya = _post(y, res[6], res[7], row3(rwkv_ln_w)[l], row3(rwkv_ln_b)[l], bd_mean, tm)
        return ya, s_fin

    for l in range(depth):
        proj_consts = (row3(norm1_g)[l], w_proj[l], bdk, bdq, kg_row[l], qg_row[l], bf_pad[l])
        ffn_tail = (row3(norm2_g)[l], wg[l], wu[l], wd[l])

        feats, kv, kvb, qf, qs, lf = _proj(xp, *proj_consts, tm_p)
        feats3 = feats.reshape(bp, tp, N_SHIFT)
        ya, s_fin = rwkv(feats3, jnp.zeros((bp, N_SHIFT), F32), jnp.zeros((bp, PAIR_ROWS, LANE), F32),
                         l, bp, tc_scan, tm_p)
        cum, cumt = _cumsum(lf.reshape(bp, tp, LANE), tri)
        kvb3 = kvb.reshape(bp, tp, KV_W)
        yb = _fox(qf.reshape(bp, tp, QPAD), kvb3, cum, cumt, tq, tk)
        yc = _sb(qs.reshape(bp, tp, QPAD), kvb3, u_sb, tq, tk)
        xp = _ffn(xp, ya, yb.reshape(np_rows, QPAD), yc.reshape(np_rows, QPAD),
                  wo_a[l], wo_b_slots[l], wo_c_slots[l], *ffn_tail, tm_ffn, tf)
        outs[0].append(kv.reshape(bp, tp, 4, H_ATT, HEAD_DIM))
        outs[1].append(lf[:, :H_ATT].reshape(bp, tp, H_ATT))
        outs[2].append(_unpair_state(s_fin))
        outs[3].append(feats3[:, -1])

        feats, kv, kvb, qf, qs, lf = _proj(xs, *proj_consts, tm_s)
        feats3 = feats.reshape(db, ts, N_SHIFT)
        ya, s_fin = rwkv(feats3, state_shift[l], _pair_state(state_wkv[l]), l, nb_s, ts_pad, tm_s)

        def score_rows(q):
            q = _dense_heads(q.reshape(db, ts, QPAD))
            q = q[:, :, None, :, :] * eye_h[None, None, :, :, None]
            return q.reshape(db, ts * SUBLANE, D_ATT)

        kvn_t = jnp.pad(jnp.transpose(kvb.reshape(db, ts, 4, D_ATT), (0, 2, 3, 1)),
                        ((0, 0), (0, 0), (0, 0), (0, page - ts)))
        lft_new = jnp.pad(jnp.swapaxes(lf.reshape(db, ts, LANE)[:, :, :SUBLANE], 1, 2),
                          ((0, 0), (0, 0), (0, page - ts)))
        yb, yc = _paged(page_table, cache_t, l, cache_lft, score_rows(qf), score_rows(qs), kvn_t, lft_new, pp)
        xs = _ffn(xs, ya, yb.reshape(ns_rows, D_ATT), yc.reshape(ns_rows, D_ATT),
                  wo_a[l], wo_b[l], wo_c[l], *ffn_tail, tm_s, tf)
        outs[4].append(kv.reshape(db, ts, 4, H_ATT, HEAD_DIM))
        outs[5].append(lf[:, :H_ATT].reshape(db, ts, H_ATT))
        outs[6].append(_unpair_state(s_fin))
        outs[7].append(feats3[:, -1])

    return (xp.reshape(bp, tp, D_MODEL), xs.reshape(db, ts, D_MODEL)) + tuple(jnp.stack(o) for o in outs)
```
